```python
import jax
import jax.numpy as jnp
from jax import lax
import numpy as np

D_MODEL = 1024
BATCH = 2
SEQ = 8192
DEPTH = 4
DEC_BATCH = 128
DEC_SEQ = 8
PAST_LEN = 8192
PAGE_SIZE = 128

MLA_HEADS = 8
MLA_NOPE = 64
MLA_ROPE = 32
MLA_V = 64
MLA_QK = MLA_NOPE + MLA_ROPE
MLA_Q_LORA = D_MODEL // 4
MLA_KV_LORA = D_MODEL // 8
LATENT = MLA_KV_LORA + MLA_ROPE
NSA_HEADS = 8
NSA_HD = 64
CMP_BLOCK = 64
SEL_BLOCK = CMP_BLOCK
SEL_TOPK = 16
WINDOW = 512
MIX_WIDTH = MLA_HEADS * MLA_V + NSA_HEADS * NSA_HD
MEM_LEN = 256
MEM_HEADS = 4
MEM_HD = 64
D_FF = 4 * D_MODEL
ROPE_THETA = 10000.0
EPS = 1e-6
Q_BLOCK = 128
IN_SIZES = (MLA_Q_LORA, MLA_KV_LORA, MLA_ROPE, NSA_HEADS * NSA_HD,
            NSA_HD, NSA_HD, NSA_HD, NSA_HD, NSA_HD, NSA_HD, NSA_HEADS * 3)
IN_WIDTH = sum(IN_SIZES)
IN_SPLITS = tuple(int(v) for v in np.cumsum(IN_SIZES)[:-1])

kernel_name = 'hymba_mla_nsa_decoder_step'


def rmsnorm(x, g):
    x32 = x.astype(jnp.float32)
    y = x32 * lax.rsqrt(jnp.mean(x32 * x32, axis=-1, keepdims=True) + EPS)
    return (y * g.astype(jnp.float32)).astype(x.dtype)


def rope(x, pos):
    half = x.shape[-1] // 2
    inv = ROPE_THETA ** (-jnp.arange(half, dtype=jnp.float32) / half)
    ang = pos.astype(jnp.float32)[..., None] * inv
    c, s = jnp.cos(ang), jnp.sin(ang)
    x32 = x.astype(jnp.float32)
    x1, x2 = x32[..., :half], x32[..., half:]
    return jnp.concatenate([x1 * c - x2 * s, x1 * s + x2 * c], axis=-1).astype(x.dtype)


def masked_softmax(s, mask):
    s = jnp.where(mask, s.astype(jnp.float32), -jnp.inf)
    m = jnp.max(s, axis=-1, keepdims=True)
    m = jnp.where(jnp.isfinite(m), m, 0.0)
    p = jnp.exp(s - m)
    return p / jnp.maximum(jnp.sum(p, axis=-1, keepdims=True), 1e-30)


def attend(q, k, v, mask):
    s = jnp.einsum('...qhd,...shd->...hqs', q, k, preferred_element_type=jnp.float32) * (q.shape[-1] ** -0.5)
    p = masked_softmax(s, mask).astype(v.dtype)
    return jnp.einsum('...hqs,...shd->...qhd', p, v)


def project(h, pos, lw):
    z = h @ lw['w_in']
    cq, ckv, kr, qn, kc, vc, ks, vs, kw, vw, gl = jnp.split(z, IN_SPLITS, axis=-1)
    hpos = pos[:, None]
    cq = rmsnorm(cq, lw['g_q_a'])
    qm = (cq @ lw['w_q_b']).reshape(*cq.shape[:-1], MLA_HEADS, MLA_QK)
    qm = rmsnorm(qm, lw['g_mla_q'])
    qm = jnp.concatenate([qm[..., :MLA_NOPE], rope(qm[..., MLA_NOPE:], hpos)], axis=-1)
    latent = jnp.concatenate([rmsnorm(ckv, lw['g_kv_a']), kr], axis=-1)
    qn = rope(rmsnorm(qn.reshape(*qn.shape[:-1], NSA_HEADS, NSA_HD), lw['g_nsa_q']), hpos)
    cmp_row = jnp.stack([kc, vc], axis=-2)
    sel_row = jnp.stack([rope(rmsnorm(ks, lw['g_sel_k']), pos), vs], axis=-2)
    win_row = jnp.stack([rope(rmsnorm(kw, lw['g_win_k']), pos), vw], axis=-2)
    gates = jax.nn.sigmoid((gl + lw['b_gate']).reshape(*gl.shape[:-1], NSA_HEADS, 3))
    return qm, latent, qn, cmp_row, sel_row, win_row, gates


def mla_kv(latent, kpos, lw):
    ckv, kr = latent[..., :MLA_KV_LORA], latent[..., MLA_KV_LORA:]
    kv = (ckv @ lw['w_kv_b']).reshape(*ckv.shape[:-1], MLA_HEADS, MLA_NOPE + MLA_V)
    k_nope, v = kv[..., :MLA_NOPE], kv[..., MLA_NOPE:]
    k_r = jnp.broadcast_to(kr[..., None, :], k_nope.shape[:-1] + (MLA_ROPE,))
    k = rmsnorm(jnp.concatenate([k_nope, k_r], axis=-1), lw['g_mla_k'])
    k = jnp.concatenate([k[..., :MLA_NOPE], rope(k[..., MLA_NOPE:], kpos[:, None])], axis=-1)
    return k, v


def nsa_compress(cmp_rows, lw):
    nb = cmp_rows.shape[-3] // CMP_BLOCK
    blocks = cmp_rows.reshape(*cmp_rows.shape[:-3], nb, CMP_BLOCK, 2, NSA_HD)
    pooled = jnp.einsum('...njcd,cj->...ncd', blocks, lw['w_cmp_pool'])
    end = jnp.arange(nb) * CMP_BLOCK + (CMP_BLOCK - 1)
    k = rope(rmsnorm(pooled[..., 0, :], lw['g_cmp_k']), end)
    return k, pooled[..., 1, :], end


def nsa_core(q, qpos, g, ck, cv, cend, sel, wk, wv, wpos):
    tq = q.shape[0]
    nb = ck.shape[0]
    scale = NSA_HD ** -0.5
    cmask = cend[None, :] <= qpos[:, None]
    s_c = jnp.einsum('qhd,nd->hqn', q, ck, preferred_element_type=jnp.float32) * scale
    p_c = masked_softmax(s_c, cmask[None])
    o_c = jnp.einsum('hqn,nd->qhd', p_c.astype(cv.dtype), cv)
    cur = (qpos // SEL_BLOCK)[:, None]
    elig = jnp.arange(nb)[None, :] < cur
    imp = jnp.where(elig, jnp.sum(p_c, axis=0), -jnp.inf)
    top_v, top_i = lax.top_k(imp, min(SEL_TOPK - 1, nb))
    idx = jnp.concatenate([cur, top_i], axis=-1)
    ok = jnp.concatenate([jnp.ones_like(cur, dtype=bool), jnp.isfinite(top_v)], axis=-1)
    n_sel = idx.shape[-1]
    gathered = sel.reshape(nb, SEL_BLOCK, 2, NSA_HD)[idx]
    kpos = idx[..., None] * SEL_BLOCK + jnp.arange(SEL_BLOCK)
    smask = ok[..., None] & (kpos <= qpos[:, None, None])
    s_s = jnp.einsum('qhd,qkjd->hqkj', q, gathered[..., 0, :], preferred_element_type=jnp.float32) * scale
    p_s = masked_softmax(s_s.reshape(NSA_HEADS, tq, n_sel * SEL_BLOCK), smask.reshape(1, tq, n_sel * SEL_BLOCK))
    v_s = gathered[..., 1, :].reshape(tq, n_sel * SEL_BLOCK, NSA_HD)
    o_s = jnp.einsum('hqm,qmd->qhd', p_s.astype(v_s.dtype), v_s)
    wmask = (wpos[None, :] <= qpos[:, None]) & (wpos[None, :] > qpos[:, None] - WINDOW) & (wpos[None, :] >= 0)
    s_w = jnp.einsum('qhd,wd->hqw', q, wk, preferred_element_type=jnp.float32) * scale
    p_w = masked_softmax(s_w, wmask[None])
    o_w = jnp.einsum('hqw,wd->qhd', p_w.astype(wv.dtype), wv)
    return g[..., 0:1] * o_c + g[..., 1:2] * o_s + g[..., 2:3] * o_w


def prompt_mixers(h, lw):
    bsz, seq, _ = h.shape
    pos = jnp.arange(seq)
    qm, latent, qn, cmp_row, sel_row, win_row, gates = project(h, pos, lw)
    k_m, v_m = mla_kv(latent, pos, lw)
    ck, cv, cend = nsa_compress(cmp_row, lw)
    win_pad = jnp.pad(win_row, ((0, 0), (WINDOW, 0), (0, 0), (0, 0)))
    win_pos = jnp.arange(seq + WINDOW) - WINDOW
    nsa_b = jax.vmap(nsa_core, in_axes=(0, None, 0, 0, 0, None, 0, 0, 0, None))

    def block(i):
        s0 = i * Q_BLOCK
        qpos = s0 + jnp.arange(Q_BLOCK)
        sl = lambda a: lax.dynamic_slice_in_dim(a, s0, Q_BLOCK, axis=1)
        o_m = attend(sl(qm), k_m, v_m, pos[None, :] <= qpos[:, None])
        w = lax.dynamic_slice_in_dim(win_pad, s0, WINDOW + Q_BLOCK, axis=1)
        wp = lax.dynamic_slice_in_dim(win_pos, s0, WINDOW + Q_BLOCK, axis=0)
        o_n = nsa_b(sl(qn), qpos, sl(gates), ck, cv, cend, sel_row, w[:, :, 0], w[:, :, 1], wp)
        return jnp.concatenate([o_m.reshape(bsz, Q_BLOCK, -1), o_n.reshape(bsz, Q_BLOCK, -1)], axis=-1)

    o = lax.map(block, jnp.arange(seq // Q_BLOCK))
    o = jnp.swapaxes(o, 0, 1).reshape(bsz, seq, MIX_WIDTH)
    n_win = min(WINDOW, seq)
    return o @ lw['w_out'], (latent, cmp_row, sel_row, win_row[:, seq - n_win:])


def sample_mixers(h, lw, layer, cache_mla, cache_nsa_cmp, cache_nsa_sel, win_buf, page_table):
    n_new = h.shape[1]
    past_len = page_table.shape[1] * cache_mla.shape[2]
    pos = past_len + jnp.arange(n_new)
    qm, latent, qn, cmp_row, sel_row, win_row, gates = project(h, pos, lw)
    total = past_len + n_new
    padded = -(-total // CMP_BLOCK) * CMP_BLOCK
    kpos = jnp.arange(total)
    w_buf = win_buf.shape[1]
    wpos = jnp.arange(past_len - w_buf, total)
    pad = ((0, padded - total), (0, 0), (0, 0))

    def one(args):
        q_m, lat_new, q_n, g_n, cmp_new, sel_new, win_new, wbuf, pages = args
        past = lambda pool: pool[layer, pages].reshape(past_len, *pool.shape[3:])
        lat = jnp.concatenate([past(cache_mla), lat_new], axis=0)
        k_m, v_m = mla_kv(lat, kpos, lw)
        o_m = attend(q_m, k_m, v_m, kpos[None, :] <= pos[:, None])
        cmp_all = jnp.pad(jnp.concatenate([past(cache_nsa_cmp), cmp_new], axis=0), pad)
        ck, cv, cend = nsa_compress(cmp_all, lw)
        sel_all = jnp.pad(jnp.concatenate([past(cache_nsa_sel), sel_new], axis=0), pad)
        win_all = jnp.concatenate([wbuf, win_new], axis=0)
        o_n = nsa_core(q_n, pos, g_n, ck, cv, cend, sel_all, win_all[:, 0], win_all[:, 1], wpos)
        o = jnp.concatenate([o_m.reshape(n_new, -1), o_n.reshape(n_new, -1)], axis=-1)
        return o, win_all[win_all.shape[0] - w_buf:]

    o, new_win = lax.map(one, (qm, latent, qn, gates, cmp_row, sel_row, win_row, win_buf, page_table))
    return o @ lw['w_out'], (latent, cmp_row, sel_row, new_win)


def mem_kv(mem_n, lw):
    lead = mem_n.shape[:-1]
    k = rmsnorm((mem_n @ lw['w_mem_k']).reshape(*lead, MEM_HEADS, MEM_HD), lw['g_mem_k'])
    v = (mem_n @ lw['w_mem_v']).reshape(*lead, MEM_HEADS, MEM_HD)
    return jnp.stack([k, v], axis=-3)


def mem_attend(h, kv, lw):
    q = rmsnorm((h @ lw['w_mem_q']).reshape(*h.shape[:-1], MEM_HEADS, MEM_HD), lw['g_mem_q'])
    o = attend(q, kv[..., 0, :, :], kv[..., 1, :, :], True)
    return o.reshape(*h.shape[:-1], MEM_HEADS * MEM_HD) @ lw['w_mem_o']


def mlp(h, lw):
    u = jax.nn.relu(h @ lw['w_up'])
    return (u * u) @ lw['w_down']


def setup_inputs(seed: int = 0) -> dict:
    key = jax.random.key(seed)
    keys = iter(jax.random.split(key, 64))
    f32 = jnp.float32

    def nrm(shape, scale=1.0):
        return jax.random.normal(next(keys), shape, f32) * scale

    def gain(n):
        return 1.0 + 0.01 * nrm((DEPTH, n))

    n_pages = PAST_LEN // PAGE_SIZE
    n_pool = (DEC_BATCH * n_pages * 5) // 4
    w_buf = min(WINDOW, PAST_LEN)
    inp = {}
    inp['x_prompt'] = nrm((BATCH, SEQ, D_MODEL))
    inp['x_sample'] = nrm((DEC_BATCH, DEC_SEQ, D_MODEL))
    inp['cache_mla'] = nrm((DEPTH, n_pool, PAGE_SIZE, LATENT))
    inp['cache_nsa_cmp'] = nrm((DEPTH, n_pool, PAGE_SIZE, 2, NSA_HD))
    inp['cache_nsa_sel'] = nrm((DEPTH, n_pool, PAGE_SIZE, 2, NSA_HD))
    inp['state_nsa_win'] = nrm((DEPTH, DEC_BATCH, w_buf, 2, NSA_HD))
    inp['cache_mem'] = nrm((DEPTH, DEC_BATCH, MEM_LEN, 2, MEM_HEADS, MEM_HD))
    perm = jax.random.permutation(next(keys), n_pool)
    inp['page_table'] = perm[: DEC_BATCH * n_pages].reshape(DEC_BATCH, n_pages).astype(jnp.int32)
    inp['mem_prompt'] = nrm((BATCH, MEM_LEN, D_MODEL))
    inp['g_mix'] = gain(D_MODEL)
    inp['w_in'] = nrm((DEPTH, D_MODEL, IN_WIDTH), D_MODEL ** -0.5)
    inp['b_gate'] = nrm((DEPTH, NSA_HEADS * 3), 0.01)
    inp['g_q_a'] = gain(MLA_Q_LORA)
    inp['w_q_b'] = nrm((DEPTH, MLA_Q_LORA, MLA_HEADS * MLA_QK), MLA_Q_LORA ** -0.5)
    inp['g_kv_a'] = gain(MLA_KV_LORA)
    inp['w_kv_b'] = nrm((DEPTH, MLA_KV_LORA, MLA_HEADS * (MLA_NOPE + MLA_V)), MLA_KV_LORA ** -0.5)
    inp['g_mla_q'] = gain(MLA_QK)
    inp['g_mla_k'] = gain(MLA_QK)
    inp['g_nsa_q'] = gain(NSA_HD)
    inp['g_cmp_k'] = gain(NSA_HD)
    inp['g_sel_k'] = gain(NSA_HD)
    inp['g_win_k'] = gain(NSA_HD)
    inp['w_cmp_pool'] = (1.0 + 0.1 * nrm((DEPTH, 2, CMP_BLOCK))) / CMP_BLOCK
    inp['w_out'] = nrm((DEPTH, MIX_WIDTH, D_MODEL), MIX_WIDTH ** -0.5)
    inp['g_mem'] = gain(D_MODEL)
    inp['g_mem_ctx'] = gain(D_MODEL)
    inp['w_mem_q'] = nrm((DEPTH, D_MODEL, MEM_HEADS * MEM_HD), D_MODEL ** -0.5)
    inp['w_mem_k'] = nrm((DEPTH, D_MODEL, MEM_HEADS * MEM_HD), D_MODEL ** -0.5)
    inp['w_mem_v'] = nrm((DEPTH, D_MODEL, MEM_HEADS * MEM_HD), D_MODEL ** -0.5)
    inp['g_mem_q'] = gain(MEM_HD)
    inp['g_mem_k'] = gain(MEM_HD)
    inp['w_mem_o'] = nrm((DEPTH, MEM_HEADS * MEM_HD, D_MODEL), (MEM_HEADS * MEM_HD) ** -0.5)
    inp['g_mlp'] = gain(D_MODEL)
    inp['w_up'] = nrm((DEPTH, D_MODEL, D_FF), D_MODEL ** -0.5)
    inp['w_down'] = nrm((DEPTH, D_FF, D_MODEL), D_FF ** -0.5)
    return inp


def reference(x_prompt, x_sample, cache_mla, cache_nsa_cmp, cache_nsa_sel, state_nsa_win, cache_mem,
              page_table, mem_prompt, g_mix, w_in, b_gate, g_q_a, w_q_b, g_kv_a, w_kv_b, g_mla_q, g_mla_k,
              g_nsa_q, g_cmp_k, g_sel_k, g_win_k, w_cmp_pool, w_out, g_mem, g_mem_ctx, w_mem_q, w_mem_k,
              w_mem_v, g_mem_q, g_mem_k, w_mem_o, g_mlp, w_up, w_down):
    xp, xs = x_prompt, x_sample
    p_mla, p_cmp, p_sel, p_win, p_mem = [], [], [], [], []
    s_mla, s_cmp, s_sel, s_win = [], [], [], []
    for l in range(DEPTH):
        lw = {
            'w_in': w_in[l], 'b_gate': b_gate[l], 'g_q_a': g_q_a[l], 'w_q_b': w_q_b[l],
            'g_kv_a': g_kv_a[l], 'w_kv_b': w_kv_b[l], 'g_mla_q': g_mla_q[l], 'g_mla_k': g_mla_k[l],
            'g_nsa_q': g_nsa_q[l], 'g_cmp_k': g_cmp_k[l], 'g_sel_k': g_sel_k[l], 'g_win_k': g_win_k[l],
            'w_cmp_pool': w_cmp_pool[l], 'w_out': w_out[l], 'w_mem_q': w_mem_q[l], 'w_mem_k': w_mem_k[l],
            'w_mem_v': w_mem_v[l], 'g_mem_q': g_mem_q[l], 'g_mem_k': g_mem_k[l], 'w_mem_o': w_mem_o[l],
            'w_up': w_up[l], 'w_down': w_down[l],
        }
        o, (lat, cmp_r, sel_r, win_r) = prompt_mixers(rmsnorm(xp, g_mix[l]), lw)
        xp = xp + o
        mkv = mem_kv(rmsnorm(mem_prompt, g_mem_ctx[l]), lw)
        xp = xp + mem_attend(rmsnorm(xp, g_mem[l]), mkv, lw)
        xp = xp + mlp(rmsnorm(xp, g_mlp[l]), lw)
        p_mla.append(lat)
        p_cmp.append(cmp_r)
        p_sel.append(sel_r)
        p_win.append(win_r)
        p_mem.append(mkv)
        o, (lat, cmp_r, sel_r, win_r) = sample_mixers(rmsnorm(xs, g_mix[l]), lw, l, cache_mla, cache_nsa_cmp,
                                                      cache_nsa_sel, state_nsa_win[l], page_table)
        xs = xs + o
        xs = xs + mem_attend(rmsnorm(xs, g_mem[l]), cache_mem[l], lw)
        xs = xs + mlp(rmsnorm(xs, g_mlp[l]), lw)
        s_mla.append(lat)
        s_cmp.append(cmp_r)
        s_sel.append(sel_r)
        s_win.append(win_r)
    return (xp, xs, jnp.stack(p_mla), jnp.stack(p_cmp), jnp.stack(p_sel), jnp.stack(p_win), jnp.stack(p_mem),
            jnp.stack(s_mla), jnp.stack(s_cmp), jnp.stack(s_sel), jnp.stack(s_win))
```

```python
import functools

import numpy as np
import jax
import jax.numpy as jnp
from jax import lax
from jax.experimental import pallas as pl
from jax.experimental.pallas import tpu as pltpu

F32 = jnp.float32
BF16 = jnp.bfloat16

MLA_HEADS = 8
MLA_NOPE = 64
MLA_ROPE = 32
MLA_V = 64
MLA_QK = MLA_NOPE + MLA_ROPE
NSA_HEADS = 8
NSA_HD = 64
CMP_BLOCK = 64
SEL_TOPK = 16
WINDOW = 512
MEM_HEADS = 4
MEM_HD = 64
ROPE_THETA = 10000.0
EPS = 1e-6
LANES = 128
NEG = -1e30
M_INIT = -1e20
VMEM_LIMIT = 56 * 1024 * 1024

SEG_CQ, SEG_CKV, SEG_KR, SEG_QN, SEG_CMP, SEG_SEL, SEG_WIN, SEG_GL, IN_PAD = 0, 256, 384, 512, 1024, 1152, 1280, 1408, 1536


def _dot(a, b):
    return jnp.dot(a, b, preferred_element_type=F32)


def _dot_nt(a, b):
    return lax.dot_general(a, b, (((1,), (1,)), ((), ())), preferred_element_type=F32)


def _dot_tn(a, b):
    return lax.dot_general(a, b, (((0,), (0,)), ((), ())), preferred_element_type=F32)


def _pick(n, cands):
    for c in cands:
        if n % c == 0:
            return c
    raise ValueError(f"no tile in {cands} divides {n}")


def _params(sem):
    return pltpu.CompilerParams(dimension_semantics=sem, vmem_limit_bytes=VMEM_LIMIT)


def _lane(shape):
    return lax.broadcasted_iota(jnp.int32, shape, len(shape) - 1)


def _rms_full(x, g, n):
    ms = jnp.sum(x * x, axis=-1, keepdims=True) * (1.0 / n)
    return x * lax.rsqrt(ms + EPS) * g


def _swap_half(x, half):
    lane = _lane(x.shape)
    up = pltpu.roll(x, LANES - half, axis=1)
    dn = pltpu.roll(x, half, axis=1)
    return jnp.where((lane % (2 * half)) < half, up, dn)


def _kv2(x):
    lo = _lane(x.shape) < 64
    sw = pltpu.roll(x, 64, axis=1)
    return jnp.concatenate([jnp.where(lo, x, sw), jnp.where(lo, sw, x)], axis=1).astype(BF16)


def _norm_rope_k64(x, g, c, s):
    lo = _lane(x.shape) < 64
    ssq = jnp.sum(jnp.where(lo, x * x, 0.0), axis=-1, keepdims=True)
    kx = x * lax.rsqrt(ssq * (1.0 / 64) + EPS) * g
    roped = kx * c + _swap_half(kx, 32) * s
    return jnp.where(lo, roped, x)


def _rows_from_t(xt):
    feat, npos = xt.shape
    if feat < LANES:
        xt = jnp.concatenate([xt, jnp.zeros((LANES - feat, npos), xt.dtype)], axis=0)
    blocks = [xt[:, LANES * k:LANES * (k + 1)].T for k in range(npos // LANES)]
    return blocks[0] if len(blocks) == 1 else jnp.concatenate(blocks, axis=0)


def _split3(p):
    hi = p.astype(BF16)
    r1 = p - hi.astype(F32)
    mid = r1.astype(BF16)
    lo = (r1 - mid.astype(F32)).astype(BF16)
    return hi, mid, lo


def _proj_kernel(x_ref, gmix_ref, win_ref, gqa_ref, wqb_ref, gkva_ref, wk_ref, wv_ref, gq_ref, gk_ref,
                 gnq_ref, gsel_ref, gwin_ref, bg_ref, c64_ref, s64_ref, cm_ref, sm_ref,
                 qm_ref, lat_ref, km_ref, vm_ref, qn_ref, cmp_ref, sel_ref, wino_ref, selkv_ref, winkv_ref,
                 gate_ref):
    x = x_ref[...]
    d = x.shape[-1]
    h = _rms_full(x, gmix_ref[...], d)
    z = _dot(h.astype(BF16), win_ref[...])
    c64, s64, cm, sm = c64_ref[...], s64_ref[...], cm_ref[...], sm_ref[...]
    lane = _lane(c64.shape)
    lo = lane < 64

    cqn = _rms_full(z[:, SEG_CQ:SEG_CKV], gqa_ref[...], SEG_CKV - SEG_CQ)
    qm = _dot(cqn.astype(BF16), wqb_ref[...])
    gq = gq_ref[...]
    for hd in range(MLA_HEADS):
        xh = qm[:, LANES * hd:LANES * (hd + 1)]
        ssq = jnp.sum(xh * xh, axis=-1, keepdims=True)
        xh = xh * lax.rsqrt(ssq * (1.0 / MLA_QK) + EPS) * gq
        xh = (xh * cm + _swap_half(xh, MLA_ROPE // 2) * sm) * (MLA_QK ** -0.5)
        qm_ref[:, LANES * hd:LANES * (hd + 1)] = xh.astype(BF16)

    ckvn = _rms_full(z[:, SEG_CKV:SEG_KR], gkva_ref[...], SEG_KR - SEG_CKV)
    krp = z[:, SEG_KR:SEG_QN]
    lat_ref[:, 0:128] = ckvn
    lat_ref[:, 128:160] = krp[:, 0:MLA_ROPE]
    ckvb = ckvn.astype(BF16)
    kn = _dot(ckvb, wk_ref[...])
    vm_ref[...] = _dot(ckvb, wv_ref[...]).astype(BF16)
    krpad = jnp.where((lane >= 64) & (lane < 96), krp, 0.0)
    gk = gk_ref[...]
    for hd in range(MLA_HEADS):
        kh = kn[:, LANES * hd:LANES * (hd + 1)] + krpad
        ssq = jnp.sum(kh * kh, axis=-1, keepdims=True)
        kh = kh * lax.rsqrt(ssq * (1.0 / MLA_QK) + EPS) * gk
        kh = kh * cm + _swap_half(kh, MLA_ROPE // 2) * sm
        km_ref[:, LANES * hd:LANES * (hd + 1)] = kh.astype(BF16)

    gnq = gnq_ref[...]
    for ch in range(NSA_HEADS // 2):
        xc = z[:, SEG_QN + LANES * ch:SEG_QN + LANES * (ch + 1)]
        sq = xc * xc
        s_lo = jnp.sum(jnp.where(lo, sq, 0.0), axis=-1, keepdims=True)
        s_hi = jnp.sum(jnp.where(lo, 0.0, sq), axis=-1, keepdims=True)
        r = jnp.where(lo, lax.rsqrt(s_lo * (1.0 / NSA_HD) + EPS), lax.rsqrt(s_hi * (1.0 / NSA_HD) + EPS))
        xc = xc * r * gnq
        xc = (xc * c64 + _swap_half(xc, NSA_HD // 2) * s64) * (NSA_HD ** -0.5)
        qn_ref[:, LANES * ch:LANES * (ch + 1)] = xc.astype(BF16)

    cmp_ref[...] = z[:, SEG_CMP:SEG_SEL]
    sel = _norm_rope_k64(z[:, SEG_SEL:SEG_WIN], gsel_ref[...], c64, s64)
    sel_ref[...] = sel
    selkv_ref[...] = _kv2(sel)
    win = _norm_rope_k64(z[:, SEG_WIN:SEG_GL], gwin_ref[...], c64, s64)
    wino_ref[...] = win
    winkv_ref[...] = _kv2(win)
    gate_ref[...] = 1.0 / (1.0 + jnp.exp(-(z[:, SEG_GL:IN_PAD] + bg_ref[...])))


def _proj_call(x, w, tabs, tm):
    t, d = x.shape
    row = lambda n: pl.BlockSpec((tm, n), lambda i: (i, 0))
    full = lambda a: pl.BlockSpec(a.shape, lambda i: (0,) * a.ndim)
    ins = [x, w["g_mix"], w["w_in"], w["g_q_a"], w["w_q_b"], w["g_kv_a"], w["wk_pad"], w["wv_pair"], w["g_mla_q"],
           w["g_mla_k"], w["g_nsa_q"], w["g_sel_k"], w["g_win_k"], w["b_gate"]]
    in_specs = [row(d)] + [full(a) for a in ins[1:]] + [row(LANES)] * 4
    outs = [(1024, BF16), (160, F32), (1024, BF16), (1024, BF16), (512, BF16), (128, F32), (128, F32), (128, F32),
            (256, BF16), (256, BF16), (128, F32)]
    return pl.pallas_call(
        _proj_kernel,
        grid=(t // tm,),
        in_specs=in_specs,
        out_specs=[row(n) for n, _ in outs],
        out_shape=[jax.ShapeDtypeStruct((t, n), dt) for n, dt in outs],
        compiler_params=_params(("parallel",)),
    )(*ins, tabs["c64"], tabs["s64"], tabs["cm"], tabs["sm"])


def _mm_res_kernel(*refs, n):
    res_ref, out_ref = refs[0], refs[-1]
    acc = res_ref[...]
    for k in range(n):
        acc = acc + _dot(refs[1 + 2 * k][...], refs[2 + 2 * k][...])
    out_ref[...] = acc


def _mm_res_call(res, pairs, tm):
    t, d = res.shape
    row = lambda n: pl.BlockSpec((tm, n), lambda i: (i, 0))
    full = lambda a: pl.BlockSpec(a.shape, lambda i: (0,) * a.ndim)
    ins, specs = [res], [row(d)]
    for a, wgt in pairs:
        ins += [a, wgt]
        specs += [row(a.shape[1]), full(wgt)]
    return pl.pallas_call(
        functools.partial(_mm_res_kernel, n=len(pairs)),
        grid=(t // tm,), in_specs=specs, out_specs=row(d),
        out_shape=jax.ShapeDtypeStruct((t, d), F32),
        compiler_params=_params(("parallel",)),
    )(*ins)


def _norm_mm_kernel(x_ref, g_ref, w_ref, o_ref):
    x = x_ref[...]
    o_ref[...] = _dot(_rms_full(x, g_ref[...], x.shape[-1]).astype(BF16), w_ref[...])


def _norm_mm_call(x, g, wgt, tm):
    t, d = x.shape
    n = wgt.shape[1]
    return pl.pallas_call(
        _norm_mm_kernel,
        grid=(t // tm,),
        in_specs=[pl.BlockSpec((tm, d), lambda i: (i, 0)), pl.BlockSpec(g.shape, lambda i: (0, 0)),
                  pl.BlockSpec(wgt.shape, lambda i: (0, 0))],
        out_specs=pl.BlockSpec((tm, n), lambda i: (i, 0)),
        out_shape=jax.ShapeDtypeStruct((t, n), F32),
        compiler_params=_params(("parallel",)),
    )(x, g, wgt)


def _head_rms64(x, g2):
    lo = _lane(x.shape) < 64
    sq = x * x
    s_lo = jnp.sum(jnp.where(lo, sq, 0.0), axis=-1, keepdims=True)
    s_hi = jnp.sum(jnp.where(lo, 0.0, sq), axis=-1, keepdims=True)
    r = jnp.where(lo, lax.rsqrt(s_lo * (1.0 / 64) + EPS), lax.rsqrt(s_hi * (1.0 / 64) + EPS))
    return x * r * g2


def _memkv_kernel(x_ref, g_ref, w_ref, gk_ref, o_ref):
    x = x_ref[...]
    z = _dot(_rms_full(x, g_ref[...], x.shape[-1]).astype(BF16), w_ref[...])
    gk = gk_ref[...]
    kw = MEM_HEADS * MEM_HD
    for ch in range(kw // LANES):
        o_ref[:, LANES * ch:LANES * (ch + 1)] = _head_rms64(z[:, LANES * ch:LANES * (ch + 1)], gk)
    o_ref[:, kw:2 * kw] = z[:, kw:2 * kw]


def _memkv_call(x, g, wkv, gk2):
    t, d = x.shape
    tm = _pick(t, (256, 128))
    n = wkv.shape[1]
    return pl.pallas_call(
        _memkv_kernel,
        grid=(t // tm,),
        in_specs=[pl.BlockSpec((tm, d), lambda i: (i, 0)), pl.BlockSpec(g.shape, lambda i: (0, 0)),
                  pl.BlockSpec(wkv.shape, lambda i: (0, 0)), pl.BlockSpec(gk2.shape, lambda i: (0, 0))],
        out_specs=pl.BlockSpec((tm, n), lambda i: (i, 0)),
        out_shape=jax.ShapeDtypeStruct((t, n), F32),
        compiler_params=_params(("parallel",)),
    )(x, g, wkv, gk2)


def _mem_attn_kernel(q_ref, kv_ref, gq_ref, o_ref, *, kv_feature_major):
    q = q_ref[0]
    tq = q.shape[0]
    kw = MEM_HEADS * MEM_HD
    gq = gq_ref[...]
    lo = _lane((tq, LANES)) < 64
    for pr in range(kw // LANES):
        qp = _head_rms64(q[:, LANES * pr:LANES * (pr + 1)], gq) * (MEM_HD ** -0.5)
        q2 = jnp.concatenate([jnp.where(lo, qp, 0.0), jnp.where(lo, 0.0, qp)], axis=0).astype(BF16)
        if kv_feature_major:
            s = _dot(q2, kv_ref[0, LANES * pr:LANES * (pr + 1), :].astype(BF16))
        else:
            s = _dot_nt(q2, kv_ref[0, :, LANES * pr:LANES * (pr + 1)].astype(BF16))
        m = jnp.max(s, axis=-1, keepdims=True)
        p = jnp.exp(s - m)
        p = (p / jnp.maximum(jnp.sum(p, axis=-1, keepdims=True), 1e-30)).astype(BF16)
        if kv_feature_major:
            o = _dot_nt(p, kv_ref[0, kw + LANES * pr:kw + LANES * (pr + 1), :].astype(BF16))
        else:
            o = _dot(p, kv_ref[0, :, kw + LANES * pr:kw + LANES * (pr + 1)].astype(BF16))
        o_ref[0, :, LANES * pr:LANES * (pr + 1)] = jnp.where(lo, o[:tq], o[tq:]).astype(BF16)


def _mem_attn_call(q, kv, gq2, tq, kv_feature_major):
    g, t, kw = q.shape
    return pl.pallas_call(
        functools.partial(_mem_attn_kernel, kv_feature_major=kv_feature_major),
        grid=(g, t // tq),
        in_specs=[pl.BlockSpec((1, tq, kw), lambda b, i: (b, i, 0)),
                  pl.BlockSpec((1,) + kv.shape[1:], lambda b, i: (b, 0, 0)),
                  pl.BlockSpec(gq2.shape, lambda b, i: (0, 0))],
        out_specs=pl.BlockSpec((1, tq, kw), lambda b, i: (b, i, 0)),
        out_shape=jax.ShapeDtypeStruct((g, t, kw), BF16),
        compiler_params=_params(("parallel", "parallel")),
    )(q, kv, gq2)


def _mlp_kernel(x_ref, g_ref, wu_ref, wd_ref, o_ref, h_ref, acc_ref):
    f = pl.program_id(1)

    @pl.when(f == 0)
    def _():
        x = x_ref[...]
        h_ref[...] = _rms_full(x, g_ref[...], x.shape[-1]).astype(BF16)
        acc_ref[...] = x

    u = jnp.maximum(_dot(h_ref[...], wu_ref[...]), 0.0)
    acc_ref[...] += _dot((u * u).astype(BF16), wd_ref[...])

    @pl.when(f == pl.num_programs(1) - 1)
    def _():
        o_ref[...] = acc_ref[...]


def _mlp_call(x, g, wu, wd, tm):
    t, d = x.shape
    ff = wu.shape[1]
    tf = _pick(ff, (1024, 512, 256, 128))
    return pl.pallas_call(
        _mlp_kernel,
        grid=(t // tm, ff // tf),
        in_specs=[pl.BlockSpec((tm, d), lambda i, f: (i, 0)), pl.BlockSpec(g.shape, lambda i, f: (0, 0)),
                  pl.BlockSpec((d, tf), lambda i, f: (0, f)), pl.BlockSpec((tf, d), lambda i, f: (f, 0))],
        out_specs=pl.BlockSpec((tm, d), lambda i, f: (i, 0)),
        out_shape=jax.ShapeDtypeStruct((t, d), F32),
        scratch_shapes=[pltpu.VMEM((tm, d), BF16), pltpu.VMEM((tm, d), F32)],
        compiler_params=_params(("parallel", "arbitrary")),
    )(x, g, wu, wd)


def _mla_attn_kernel(q_ref, k_ref, v_ref, o_ref, *, tq):
    i = pl.program_id(2)
    row = lax.broadcasted_iota(jnp.int32, (tq, tq), 0)
    col = lax.broadcasted_iota(jnp.int32, (tq, tq), 1)
    out = None
    for hh in range(2):
        q = q_ref[0, :, LANES * hh:LANES * (hh + 1)]

        def step(j, carry, masked, hh=hh, q=q):
            m, l, acc = carry
            start = pl.multiple_of(j * tq, tq)
            k = k_ref[0, pl.ds(start, tq), LANES * hh:LANES * (hh + 1)]
            v = v_ref[0, pl.ds(start, tq), LANES * hh:LANES * (hh + 1)]
            s = _dot_nt(q, k)
            if masked:
                s = jnp.where(col <= row, s, NEG)
            m_new = jnp.maximum(m, jnp.max(s, axis=-1, keepdims=True))
            p = jnp.exp(s - m_new)
            alpha = jnp.exp(m - m_new)
            l = alpha * l + jnp.sum(p, axis=-1, keepdims=True)
            acc = alpha * acc + _dot(p.astype(BF16), v)
            return m_new, l, acc

        init = (jnp.full((tq, 1), M_INIT, F32), jnp.zeros((tq, 1), F32), jnp.zeros((tq, LANES), F32))
        carry = lax.fori_loop(0, i, functools.partial(step, masked=False), init)
        m, l, acc = step(i, carry, True)
        o = acc / l
        out = o if out is None else out + o
    o_ref[0] = out.astype(BF16)


def _mla_attn_call(qm, km, vm, tq):
    b, s, _ = qm.shape
    npair = MLA_HEADS // 2
    return pl.pallas_call(
        functools.partial(_mla_attn_kernel, tq=tq),
        grid=(b, npair, s // tq),
        in_specs=[pl.BlockSpec((1, tq, 2 * LANES), lambda bb, p, i: (bb, i, p)),
                  pl.BlockSpec((1, s, 2 * LANES), lambda bb, p, i: (bb, 0, p)),
                  pl.BlockSpec((1, s, 2 * LANES), lambda bb, p, i: (bb, 0, p))],
        out_specs=pl.BlockSpec((1, tq, LANES), lambda bb, p, i: (bb, i, p)),
        out_shape=jax.ShapeDtypeStruct((b, s, npair * LANES), BF16),
        compiler_params=_params(("parallel", "parallel", "arbitrary")),
    )(qm, km, vm)


def _cmp_pool_kernel(*refs, n_in, paged):
    if paged:
        refs = refs[1:]
    x_refs = refs[:n_in]
    wp_ref, g_ref, c_ref, s_ref, o_ref = refs[n_in:]
    if paged:
        parts = [_rows_from_t(xr[0, 0]) for xr in x_refs]
    else:
        parts = [xr[0] for xr in x_refs]
    x = parts[0] if len(parts) == 1 else jnp.concatenate(parts, axis=0)
    nb = x.shape[0] // CMP_BLOCK
    pooled = jnp.sum(x.reshape(nb, CMP_BLOCK, LANES) * wp_ref[...][None], axis=1)
    o_ref[0] = _norm_rope_k64(pooled, g_ref[...], c_ref[...], s_ref[...])


def _cmp_pool_prompt_call(cmp_rows, wp, g, c_end, s_end):
    b, s, _ = cmp_rows.shape
    nb = s // CMP_BLOCK
    full = lambda a: pl.BlockSpec(a.shape, lambda i: (0,) * a.ndim)
    return pl.pallas_call(
        functools.partial(_cmp_pool_kernel, n_in=1, paged=False),
        grid=(b,),
        in_specs=[pl.BlockSpec((1, s, LANES), lambda i: (i, 0, 0)), full(wp), full(g), full(c_end), full(s_end)],
        out_specs=pl.BlockSpec((1, nb, LANES), lambda i: (i, 0, 0)),
        out_shape=jax.ShapeDtypeStruct((b, nb, LANES), F32),
        compiler_params=_params(("parallel",)),
    )(cmp_rows, wp, g, c_end, s_end)


def _page_specs(layer, pages_per_step, feat, page_rows):
    return [pl.BlockSpec((1, 1, feat, page_rows),
                         lambda s, c, pt, i=i: (layer, pt[s, c * pages_per_step + i], 0, 0))
            for i in range(pages_per_step)]


def _cmp_pool_sample_call(page_table, cache, layer, wp, g, c_end, s_end, pps):
    n_seq, n_pages = page_table.shape
    page_rows = cache.shape[3]
    bps = pps * page_rows // CMP_BLOCK
    nb = n_pages * page_rows // CMP_BLOCK
    full = lambda a: pl.BlockSpec(a.shape, lambda s, c, pt: (0,) * a.ndim)
    grid_spec = pltpu.PrefetchScalarGridSpec(
        num_scalar_prefetch=1,
        grid=(n_seq, n_pages // pps),
        in_specs=_page_specs(layer, pps, LANES, page_rows) + [
            full(wp), full(g), pl.BlockSpec((bps, LANES), lambda s, c, pt: (c, 0)),
            pl.BlockSpec((bps, LANES), lambda s, c, pt: (c, 0))],
        out_specs=pl.BlockSpec((1, bps, LANES), lambda s, c, pt: (s, c, 0)),
    )
    return pl.pallas_call(
        functools.partial(_cmp_pool_kernel, n_in=pps, paged=True),
        grid_spec=grid_spec,
        out_shape=jax.ShapeDtypeStruct((n_seq, nb, LANES), F32),
        compiler_params=_params(("parallel", "arbitrary")),
    )(page_table, *([cache] * pps), wp, g, c_end, s_end)


def _softmax_rows(s, mask):
    s = jnp.where(mask, s, NEG)
    m = jnp.max(s, axis=-1, keepdims=True)
    p = jnp.where(mask, jnp.exp(s - m), 0.0)
    return p / jnp.maximum(jnp.sum(p, axis=-1, keepdims=True), 1e-30)


def _topk_mask(imp, elig, k, axis):
    n = imp.shape[axis]
    idx = lax.broadcasted_iota(jnp.int32, imp.shape, axis)
    work = jnp.where(elig, imp, -jnp.inf)
    sel = jnp.zeros(imp.shape, F32)
    for _ in range(k):
        mx = jnp.max(work, axis=axis, keepdims=True)
        is_max = (work == mx) & (mx > -jnp.inf)
        first = jnp.min(jnp.where(is_max, idx, n), axis=axis, keepdims=True)
        pick = idx == first
        sel = jnp.where(pick, 1.0, sel)
        work = jnp.where(pick, -jnp.inf, work)
    return sel


def _nsa_prompt_kernel(q_ref, gate_ref, cmp_ref, selkv_ref, winkv_ref, e_ref, o_ref, *, tq, tk):
    i = pl.program_id(1)
    s0 = i * tq
    nh = NSA_HEADS
    nb = cmp_ref.shape[1]
    lo = _lane((tq, LANES)) < 64
    q = q_ref[0]
    parts = []
    for pr in range(nh // 2):
        qp = q[:, LANES * pr:LANES * (pr + 1)]
        parts += [jnp.where(lo, qp, 0), jnp.where(lo, 0, qp)]
    q8 = jnp.concatenate(parts, axis=0).astype(BF16)
    qpos = s0 + lax.broadcasted_iota(jnp.int32, (tq, 1), 0)

    ckv = _kv2(cmp_ref[0])
    blk = lax.broadcasted_iota(jnp.int32, (tq, nb), 1)
    cmask = (blk * CMP_BLOCK + (CMP_BLOCK - 1)) <= qpos
    s_c = _dot_nt(q8, ckv[:, 0:LANES]).reshape(nh, tq, nb)
    p_c = _softmax_rows(s_c, cmask[None])
    o_c = _dot(p_c.reshape(nh * tq, nb).astype(BF16), ckv[:, LANES:2 * LANES])
    imp = p_c[0]
    for hd in range(1, nh):
        imp = imp + p_c[hd]

    cur = qpos // CMP_BLOCK
    selmask = jnp.maximum(_topk_mask(imp, blk < cur, min(SEL_TOPK - 1, nb), 1), (blk == cur).astype(F32))
    selb = selmask.astype(BF16)

    def sel_step(j, carry):
        m, l, acc = carry
        start = pl.multiple_of(j * tk, tk)
        k2 = selkv_ref[0, pl.ds(start, tk), 0:LANES]
        v2 = selkv_ref[0, pl.ds(start, tk), LANES:2 * LANES]
        allow = _dot(selb, e_ref[j])
        kpos = start + lax.broadcasted_iota(jnp.int32, (1, tk), 1)
        bias = jnp.where((allow > 0.5) & (kpos <= qpos), 0.0, NEG)
        s = _dot_nt(q8, k2).reshape(nh, tq, tk) + bias[None]
        m_new = jnp.maximum(m, jnp.max(s, axis=-1, keepdims=True))
        p = jnp.exp(s - m_new)
        alpha = jnp.exp(m - m_new)
        l = alpha * l + jnp.sum(p, axis=-1, keepdims=True)
        pv = _dot(p.reshape(nh * tq, tk).astype(BF16), v2).reshape(nh, tq, LANES)
        return m_new, l, alpha * acc + pv

    init = (jnp.full((nh, tq, 1), M_INIT, F32), jnp.zeros((nh, tq, 1), F32), jnp.zeros((nh, tq, LANES), F32))
    n_tiles = (s0 + tq + tk - 1) // tk
    _, l_s, acc_s = lax.fori_loop(0, n_tiles, sel_step, init)
    o_s = (acc_s / l_s).reshape(nh * tq, LANES)

    wlen = WINDOW + tq
    wstart = pl.multiple_of(jnp.maximum(s0 - WINDOW, 0), tq)
    wk = winkv_ref[0, pl.ds(wstart, wlen), 0:LANES]
    wv = winkv_ref[0, pl.ds(wstart, wlen), LANES:2 * LANES]
    wpos = wstart + lax.broadcasted_iota(jnp.int32, (1, wlen), 1)
    wmask = (wpos <= qpos) & (wpos > qpos - WINDOW)
    p_w = _softmax_rows(_dot_nt(q8, wk).reshape(nh, tq, wlen), wmask[None])
    o_w = _dot(p_w.reshape(nh * tq, wlen).astype(BF16), wv)

    gate = gate_ref[0]
    gcols = [[gate[:, 3 * hd + c:3 * hd + c + 1] for hd in range(nh)] for c in range(3)]
    g_c, g_s, g_w = [jnp.concatenate(cols, axis=0) for cols in gcols]
    mixed = g_c * o_c + g_s * o_s + g_w * o_w
    for pr in range(nh // 2):
        ev = mixed[(2 * pr) * tq:(2 * pr + 1) * tq]
        od = mixed[(2 * pr + 1) * tq:(2 * pr + 2) * tq]
        o_ref[0, :, LANES * pr:LANES * (pr + 1)] = jnp.where(lo, ev, od).astype(BF16)


def _nsa_prompt_call(qn, gates, cmpx, selkv, winkv, e_exp, tq, tk):
    b, s, _ = qn.shape
    nb = cmpx.shape[1]
    return pl.pallas_call(
        functools.partial(_nsa_prompt_kernel, tq=tq, tk=tk),
        grid=(b, s // tq),
        in_specs=[pl.BlockSpec((1, tq, 512), lambda bb, i: (bb, i, 0)),
                  pl.BlockSpec((1, tq, LANES), lambda bb, i: (bb, i, 0)),
                  pl.BlockSpec((1, nb, LANES), lambda bb, i: (bb, 0, 0)),
                  pl.BlockSpec((1, s, 2 * LANES), lambda bb, i: (bb, 0, 0)),
                  pl.BlockSpec((1, s, 2 * LANES), lambda bb, i: (bb, 0, 0)),
                  pl.BlockSpec(e_exp.shape, lambda bb, i: (0, 0, 0))],
        out_specs=pl.BlockSpec((1, tq, 512), lambda bb, i: (bb, i, 0)),
        out_shape=jax.ShapeDtypeStruct((b, s, 512), BF16),
        compiler_params=_params(("parallel", "arbitrary")),
    )(qn, gates, cmpx, selkv, winkv, e_exp)


def _online_update(s, val_b, m_ref, l_ref, acc_ref):
    m_old = m_ref[...]
    m_new = jnp.maximum(m_old, jnp.max(s, axis=0, keepdims=True))
    p = jnp.exp(s - m_new)
    alpha = jnp.exp(m_old - m_new)
    l_ref[...] = alpha * l_ref[...] + jnp.sum(p, axis=0, keepdims=True)
    acc_ref[...] = alpha * acc_ref[...] + _dot_tn(val_b, p.astype(BF16))
    m_ref[...] = m_new


def _softmax_cols(s, mask):
    s = jnp.where(mask, s, NEG)
    m = jnp.max(s, axis=0, keepdims=True)
    p = jnp.where(mask, jnp.exp(s - m), 0.0)
    return p / jnp.maximum(jnp.sum(p, axis=0, keepdims=True), 1e-30)


def _fold_heads(y, n_tok):
    rows, width = y.shape
    r = lax.broadcasted_iota(jnp.int32, y.shape, 0)
    c = lax.broadcasted_iota(jnp.int32, y.shape, 1)
    y = jnp.where((c // 64) == (r // n_tok), y, 0.0)
    return jnp.sum(y.reshape(rows // n_tok, n_tok, width), axis=0)


def _head_rows(q, width):
    n_tok, tot = q.shape
    nh = tot // width
    rep = jnp.concatenate([q] * nh, axis=0)
    r = lax.broadcasted_iota(jnp.int32, rep.shape, 0)
    c = lax.broadcasted_iota(jnp.int32, rep.shape, 1)
    return jnp.where((c // width) == (r // n_tok), rep, jnp.zeros_like(rep))


def _mla_decode_kernel(*refs, pps):
    refs = refs[1:]
    q_ref, latnew_ref = refs[0], refs[1]
    page_refs = refs[2:2 + pps]
    (tc_ref, ts_ref, tcn_ref, tsn_ref, wk_ref, wkt_ref, wv_ref, ones_ref, seln_ref, selr_ref, selb_ref, gn_ref,
     gr_ref, o_ref, a1_ref, a2_ref, b2_ref, m_ref, l_ref, acc_ref) = refs[2 + pps:]
    c = pl.program_id(1)
    n_tok = q_ref.shape[1]

    @pl.when(c == 0)
    def _():
        qbd = _head_rows(q_ref[0], LANES)
        qn = _dot(qbd, seln_ref[...]) * gn_ref[...]
        a1_ref[...] = _dot(qn.astype(BF16), wkt_ref[...]).astype(BF16)
        a2_ref[...] = (_dot(qbd, selr_ref[...]) * gr_ref[...]).astype(BF16)
        b2_ref[...] = (_dot(qbd, selb_ref[...]) * gr_ref[...]).astype(BF16)
        m_ref[...] = jnp.full(m_ref.shape, M_INIT, F32)
        l_ref[...] = jnp.zeros(l_ref.shape, F32)
        acc_ref[...] = jnp.zeros(acc_ref.shape, F32)

    def chunk(ckv, kr, c2, s2, mask):
        ckvb = ckv.astype(BF16)
        kn = _dot(ckvb, wk_ref[...])
        ssq = _dot((kn * kn).astype(BF16), ones_ref[...])
        krsq = jnp.sum(kr * kr, axis=-1, keepdims=True)
        r = lax.rsqrt((ssq + krsq) * (1.0 / MLA_QK) + EPS)
        s = (_dot_nt(ckvb, a1_ref[...]) + _dot_nt((kr * c2).astype(BF16), a2_ref[...])
             + _dot_nt((kr * s2).astype(BF16), b2_ref[...])) * r
        if mask is not None:
            s = jnp.where(mask, s, NEG)
        _online_update(s, ckvb, m_ref, l_ref, acc_ref)

    ckv = jnp.concatenate([_rows_from_t(pr[0, 0, 0:128, :]) for pr in page_refs], axis=0)
    kr = jnp.concatenate([_rows_from_t(pr[0, 0, 128:128 + MLA_ROPE, :]) for pr in page_refs], axis=0)
    chunk(ckv, kr[:, 0:MLA_ROPE], tc_ref[...], ts_ref[...], None)

    @pl.when(c == pl.num_programs(1) - 1)
    def _():
        rr = lax.broadcasted_iota(jnp.int32, (n_tok, MLA_HEADS * n_tok), 0)
        cc = lax.broadcasted_iota(jnp.int32, (n_tok, MLA_HEADS * n_tok), 1)
        latn = latnew_ref[0]
        chunk(latn[:, 0:128], latn[:, 128:128 + MLA_ROPE], tcn_ref[...], tsn_ref[...], rr <= (cc % n_tok))
        olat = (acc_ref[...] / l_ref[...]).astype(BF16)
        res = _dot_tn(olat, wv_ref[...])
        o_ref[0] = _fold_heads(res, n_tok).astype(BF16)


def _mla_decode_call(page_table, qm, latnew, cache, layer, tabs, w, consts, pps):
    n_seq, n_pages = page_table.shape
    lw, page_rows = cache.shape[2], cache.shape[3]
    n_tok = qm.shape[1]
    hq = MLA_HEADS * n_tok
    rows = pps * page_rows
    full = lambda a: pl.BlockSpec(a.shape, lambda s, c, pt: (0,) * a.ndim)
    seq3 = lambda a: pl.BlockSpec((1,) + a.shape[1:], lambda s, c, pt: (s, 0, 0))
    consts_in = [tabs["tcn"], tabs["tsn"], w["wk_c"], w["wk_ct"], w["wv_c"], consts["ones64"], consts["seln"],
                 consts["selr"], consts["selb"], w["g_kn"], w["g_kr"]]
    grid_spec = pltpu.PrefetchScalarGridSpec(
        num_scalar_prefetch=1,
        grid=(n_seq, n_pages // pps),
        in_specs=[seq3(qm), seq3(latnew)] + _page_specs(layer, pps, lw, page_rows) + [
            pl.BlockSpec((rows, MLA_ROPE), lambda s, c, pt: (c, 0)),
            pl.BlockSpec((rows, MLA_ROPE), lambda s, c, pt: (c, 0))] + [full(a) for a in consts_in],
        out_specs=pl.BlockSpec((1, n_tok, MLA_HEADS * MLA_V), lambda s, c, pt: (s, 0, 0)),
        scratch_shapes=[pltpu.VMEM((hq, 128), BF16), pltpu.VMEM((hq, MLA_ROPE), BF16), pltpu.VMEM((hq, MLA_ROPE), BF16),
                        pltpu.VMEM((1, hq), F32), pltpu.VMEM((1, hq), F32), pltpu.VMEM((128, hq), F32)],
    )
    return pl.pallas_call(
        functools.partial(_mla_decode_kernel, pps=pps),
        grid_spec=grid_spec,
        out_shape=jax.ShapeDtypeStruct((n_seq, n_tok, MLA_HEADS * MLA_V), BF16),
        compiler_params=_params(("parallel", "arbitrary")),
    )(page_table, qm, latnew, *([cache] * pps), tabs["tc"], tabs["ts"], *consts_in)


def _nsa_decode_kernel(*refs, pps, past_len):
    refs = refs[1:]
    q_ref, gate_ref, cmp_ref, wbuf_ref, winnew_ref, selnew_ref = refs[:6]
    page_refs = refs[6:6 + pps]
    (e_ref, sel512_ref, hsum_ref, rep_ref, pick_ref, o_ref, qrow_ref, selt_ref, oc_ref, ow_ref, m_ref, l_ref,
     acc_ref) = refs[6 + pps:]
    c = pl.program_id(1)
    n_tok = q_ref.shape[1]
    hq = NSA_HEADS * n_tok
    nb = cmp_ref.shape[1]

    def lane_tok(shape):
        return lax.broadcasted_iota(jnp.int32, shape, 1) % n_tok

    @pl.when(c == 0)
    def _():
        qrows = _dot(_head_rows(q_ref[0], NSA_HD), sel512_ref[...]).astype(BF16)
        qrow_ref[...] = qrows
        xc = cmp_ref[0].astype(BF16)
        blk = lax.broadcasted_iota(jnp.int32, (nb, hq), 0)
        qpos = past_len + lane_tok((nb, hq))
        p_c = _softmax_cols(_dot_nt(xc, qrows), (blk * CMP_BLOCK + (CMP_BLOCK - 1)) <= qpos)
        oc_ref[...] = _dot_tn(xc, p_c.astype(BF16))[64:128]
        hi, mid, low = _split3(p_c)
        hs = hsum_ref[...]
        imp = _dot(hi, hs) + _dot(mid, hs) + _dot(low, hs)
        selt_ref[...] = _topk_mask(imp, blk < (qpos // CMP_BLOCK), min(SEL_TOPK - 1, nb), 0)
        xw = jnp.concatenate([_rows_from_t(wbuf_ref[0, 0]), winnew_ref[0]], axis=0)
        wl = xw.shape[0]
        wpos = (past_len - wbuf_ref.shape[3]) + lax.broadcasted_iota(jnp.int32, (wl, hq), 0)
        qposw = past_len + lane_tok((wl, hq))
        xwb = xw.astype(BF16)
        p_w = _softmax_cols(_dot_nt(xwb, qrows), (wpos <= qposw) & (wpos > qposw - WINDOW) & (wpos >= 0))
        ow_ref[...] = _dot_tn(xwb, p_w.astype(BF16))[64:128]
        m_ref[...] = jnp.full(m_ref.shape, M_INIT, F32)
        l_ref[...] = jnp.zeros(l_ref.shape, F32)
        acc_ref[...] = jnp.zeros(acc_ref.shape, F32)

    xs = jnp.concatenate([_rows_from_t(pr[0, 0]) for pr in page_refs], axis=0).astype(BF16)
    allow = _dot(e_ref[0], selt_ref[...].astype(BF16))
    s = _dot_nt(xs, qrow_ref[...])
    _online_update(jnp.where(allow > 0.5, s, NEG), xs, m_ref, l_ref, acc_ref)

    @pl.when(c == pl.num_programs(1) - 1)
    def _():
        xn = selnew_ref[0].astype(BF16)
        rr = lax.broadcasted_iota(jnp.int32, (n_tok, hq), 0)
        sn = _dot_nt(xn, qrow_ref[...])
        _online_update(jnp.where(rr <= lane_tok((n_tok, hq)), sn, NEG), xn, m_ref, l_ref, acc_ref)
        o_s = acc_ref[...][64:128] / l_ref[...]
        gate = gate_ref[0]
        ghi = gate.astype(BF16)
        glo = (gate - ghi.astype(F32)).astype(BF16)
        diag = lax.broadcasted_iota(jnp.int32, (n_tok, hq), 0) == lane_tok((n_tok, hq))
        gs = []
        for b in range(3):
            rb = _dot(ghi, pick_ref[b]) + _dot(glo, pick_ref[b])
            gs.append(jnp.sum(jnp.where(diag, rb, 0.0), axis=0, keepdims=True))
        mixed = gs[0] * oc_ref[...] + gs[1] * o_s + gs[2] * ow_ref[...]
        ytile = _dot_tn(mixed.astype(BF16), rep_ref[...])
        o_ref[0] = _fold_heads(ytile, n_tok).astype(BF16)


def _nsa_decode_call(page_table, qn, gates, cmpx, win_state, layer, winnew, selnew, cache, consts, pps, past_len):
    n_seq, n_pages = page_table.shape
    page_rows = cache.shape[3]
    n_tok = qn.shape[1]
    hq = NSA_HEADS * n_tok
    nb = cmpx.shape[1]
    rows = pps * page_rows
    full = lambda a: pl.BlockSpec(a.shape, lambda s, c, pt: (0,) * a.ndim)
    seq3 = lambda a: pl.BlockSpec((1,) + a.shape[1:], lambda s, c, pt: (s, 0, 0))
    consts_in = [consts["sel512"], consts["hsum"], consts["rep"], consts["pick"]]
    grid_spec = pltpu.PrefetchScalarGridSpec(
        num_scalar_prefetch=1,
        grid=(n_seq, n_pages // pps),
        in_specs=[seq3(qn), seq3(gates), seq3(cmpx),
                  pl.BlockSpec((1, 1) + win_state.shape[2:], lambda s, c, pt: (layer, s, 0, 0)),
                  seq3(winnew), seq3(selnew)] + _page_specs(layer, pps, LANES, page_rows) + [
            pl.BlockSpec((1, rows, nb), lambda s, c, pt: (c, 0, 0))] + [full(a) for a in consts_in],
        out_specs=pl.BlockSpec((1, n_tok, NSA_HEADS * NSA_HD), lambda s, c, pt: (s, 0, 0)),
        scratch_shapes=[pltpu.VMEM((hq, LANES), BF16), pltpu.VMEM((nb, hq), F32), pltpu.VMEM((64, hq), F32),
                        pltpu.VMEM((64, hq), F32), pltpu.VMEM((1, hq), F32), pltpu.VMEM((1, hq), F32),
                        pltpu.VMEM((LANES, hq), F32)],
    )
    return pl.pallas_call(
        functools.partial(_nsa_decode_kernel, pps=pps, past_len=past_len),
        grid_spec=grid_spec,
        out_shape=jax.ShapeDtypeStruct((n_seq, n_tok, NSA_HEADS * NSA_HD), BF16),
        compiler_params=_params(("parallel", "arbitrary")),
    )(page_table, qn, gates, cmpx, win_state, winnew, selnew, *([cache] * pps), consts["e_dec"], *consts_in)


def _rope_tables(pos, half):
    inv = ROPE_THETA ** (-jnp.arange(half, dtype=F32) / half)
    ang = pos.astype(F32)[:, None] * inv
    return jnp.cos(ang), jnp.sin(ang)


def _tables64(pos):
    c, s = _rope_tables(pos, NSA_HD // 2)
    return jnp.concatenate([c, c, c, c], axis=1), jnp.concatenate([-s, s, -s, s], axis=1)


def _tables_mla(pos):
    c, s = _rope_tables(pos, MLA_ROPE // 2)
    n = pos.shape[0]
    cm = jnp.concatenate([jnp.ones((n, 64), F32), c, c, jnp.ones((n, 32), F32)], axis=1)
    sm = jnp.concatenate([jnp.zeros((n, 64), F32), -s, s, jnp.zeros((n, 32), F32)], axis=1)
    return cm, sm


def _np_consts(n_tok, nb_prompt, seq, tk, nb_past, past_len, dec_rows):
    hq = 8 * n_tok
    ones64 = np.zeros((512, hq), np.float32)
    seln = np.zeros((1024, 512), np.float32)
    selr = np.zeros((1024, MLA_ROPE), np.float32)
    selb = np.zeros((1024, MLA_ROPE), np.float32)
    sel512 = np.zeros((512, LANES), np.float32)
    rep = np.zeros((64, 512), np.float32)
    hsum = np.zeros((hq, hq), np.float32)
    pick = np.zeros((3, LANES, hq), np.float32)
    for h in range(8):
        ones64[64 * h:64 * (h + 1), n_tok * h:n_tok * (h + 1)] = 1.0
        for dd in range(64):
            seln[128 * h + dd, 64 * h + dd] = 1.0
            sel512[64 * h + dd, dd] = 1.0
            rep[dd, 64 * h + dd] = 1.0
        for e in range(MLA_ROPE):
            selr[128 * h + 64 + e, e] = 1.0
            if e < MLA_ROPE // 2:
                selb[128 * h + 64 + e + MLA_ROPE // 2, e] = 1.0
            else:
                selb[128 * h + 64 + e - MLA_ROPE // 2, e] = -1.0
        for b in range(3):
            pick[b, 3 * h + b, n_tok * h:n_tok * (h + 1)] = 1.0
        for t in range(n_tok):
            hsum[n_tok * h + t, t::n_tok] = 1.0
    key_blk = np.arange(seq) // CMP_BLOCK
    e_prompt = (np.arange(nb_prompt)[None, :, None] == key_blk.reshape(seq // tk, 1, tk)).astype(np.float32)
    key_blk_d = np.arange(past_len) // CMP_BLOCK
    e_dec = (key_blk_d.reshape(past_len // dec_rows, dec_rows, 1) == np.arange(nb_past)[None, None, :]).astype(np.float32)
    raw = dict(ones64=ones64, seln=seln, selr=selr, selb=selb, sel512=sel512, rep=rep, hsum=hsum, pick=pick,
               e_prompt=e_prompt, e_dec=e_dec)
    return {k: jnp.asarray(v, BF16) for k, v in raw.items()}


def _prep_weights(w_in, w_q_b, w_kv_b, w_out, w_mem_q, w_mem_k, w_mem_v, w_mem_o, w_up, w_down, w_cmp_pool):
    depth, d, _ = w_in.shape
    z = lambda n: jnp.zeros((depth, d, n), w_in.dtype)
    kr = w_in[..., 384:416]
    win = jnp.concatenate([w_in[..., 0:384], kr, z(32), kr, z(32), w_in[..., 416:1336], z(IN_PAD - SEG_GL - 24)], axis=-1)
    wqb = jnp.pad(w_q_b.reshape(depth, -1, MLA_HEADS, MLA_QK), ((0, 0), (0, 0), (0, 0), (0, LANES - MLA_QK)))
    wqb = wqb.reshape(depth, -1, MLA_HEADS * LANES)
    kvb = w_kv_b.reshape(depth, -1, MLA_HEADS, MLA_NOPE + MLA_V)
    wk, wv = kvb[..., :MLA_NOPE], kvb[..., MLA_NOPE:]
    r = wk.shape[1]
    wk_pad = jnp.pad(wk, ((0, 0), (0, 0), (0, 0), (0, LANES - MLA_NOPE))).reshape(depth, r, -1)
    wv_even = jnp.pad(wv, ((0, 0), (0, 0), (0, 0), (0, LANES - MLA_V)))
    wv_odd = jnp.pad(wv, ((0, 0), (0, 0), (0, 0), (LANES - MLA_V, 0)))
    odd = (jnp.arange(MLA_HEADS) % 2 == 1)[None, None, :, None]
    wv_pair = jnp.where(odd, wv_odd, wv_even).reshape(depth, r, -1)
    wk_c = wk.reshape(depth, r, -1)
    wp = jnp.concatenate([jnp.repeat(w_cmp_pool[:, 0, :, None], NSA_HD, axis=-1),
                          jnp.repeat(w_cmp_pool[:, 1, :, None], NSA_HD, axis=-1)], axis=-1)
    b16 = lambda a: a.astype(BF16)
    return dict(w_in=b16(win), w_q_b=b16(wqb), wk_pad=b16(wk_pad), wv_pair=b16(wv_pair), wk_c=b16(wk_c),
                wk_ct=b16(jnp.swapaxes(wk_c, 1, 2)), wv_c=b16(wv.reshape(depth, r, -1)),
                w_out=b16(w_out), w_mem_q=b16(w_mem_q), w_mem_kv=b16(jnp.concatenate([w_mem_k, w_mem_v], axis=-1)),
                w_mem_o=b16(w_mem_o), w_up=b16(w_up), w_down=b16(w_down), w_pool=wp)


def _pad_lanes(g, n):
    return jnp.pad(g, ((0, 0), (0, n - g.shape[1])))


def kernel(x_prompt, x_sample, cache_mla, cache_nsa_cmp, cache_nsa_sel, state_nsa_win, cache_mem, page_table,
           mem_prompt, g_mix, w_in, b_gate, g_q_a, w_q_b, g_kv_a, w_kv_b, g_mla_q, g_mla_k, g_nsa_q, g_cmp_k,
           g_sel_k, g_win_k, w_cmp_pool, w_out, g_mem, g_mem_ctx, w_mem_q, w_mem_k, w_mem_v, g_mem_q, g_mem_k,
           w_mem_o, g_mlp, w_up, w_down):
    bsz, seq, d = x_prompt.shape
    n_seq, n_tok, _ = x_sample.shape
    depth = w_in.shape[0]
    n_pages, page_rows = page_table.shape[1], cache_mla.shape[2]
    past_len = n_pages * page_rows
    w_buf = state_nsa_win.shape[2]
    mem_len = mem_prompt.shape[1]
    n_win = min(WINDOW, seq)
    assert seq % 128 == 0 and seq >= WINDOW + 128 and past_len % CMP_BLOCK == 0 and n_tok == 8
    tp, ts_ = bsz * seq, n_seq * n_tok
    t_all = tp + ts_
    tm = _pick(t_all, (512, 256, 128))
    tm_mlp = _pick(t_all, (1024, 512, 256, 128))
    tq_mla = _pick(seq, (512, 256, 128))
    tq_nsa, tk_nsa = 128, _pick(seq, (512, 256, 128))
    pps = _pick(n_pages, (8, 4, 2, 1))
    nb_p, nb_s = seq // CMP_BLOCK, past_len // CMP_BLOCK

    pos_p = jnp.tile(jnp.arange(seq), bsz)
    pos_s = jnp.tile(past_len + jnp.arange(n_tok), n_seq)
    pos_all = jnp.concatenate([pos_p, pos_s])
    c64, s64 = _tables64(pos_all)
    cm, sm = _tables_mla(pos_all)
    tabs = dict(c64=c64, s64=s64, cm=cm, sm=sm)
    ce_p, se_p = _tables64(jnp.arange(nb_p) * CMP_BLOCK + (CMP_BLOCK - 1))
    ce_s, se_s = _tables64(jnp.arange(nb_s) * CMP_BLOCK + (CMP_BLOCK - 1))
    c16, s16 = _rope_tables(jnp.arange(past_len + n_tok), MLA_ROPE // 2)
    c32, s32 = jnp.concatenate([c16, c16], axis=1), jnp.concatenate([s16, s16], axis=1)
    dtabs = dict(tc=c32[:past_len], ts=s32[:past_len], tcn=c32[past_len:], tsn=s32[past_len:])
    consts = _np_consts(n_tok, nb_p, seq, tk_nsa, nb_s, past_len, pps * page_rows)

    pw = _prep_weights(w_in, w_q_b, w_kv_b, w_out, w_mem_q, w_mem_k, w_mem_v, w_mem_o, w_up, w_down, w_cmp_pool)
    gk_pad = _pad_lanes(g_mla_k, LANES)
    gq_pad = _pad_lanes(g_mla_q, LANES)
    two = lambda g: jnp.concatenate([g, g], axis=1)
    bg_pad = _pad_lanes(b_gate, LANES)
    gn_tiled = jnp.tile(g_mla_k[:, :MLA_NOPE], (1, MLA_HEADS))

    n_pool = cache_mla.shape[1]
    cache_mla4 = jnp.swapaxes(cache_mla, 2, 3)
    cache_cmp4 = jnp.transpose(cache_nsa_cmp, (0, 1, 3, 4, 2)).reshape(depth, n_pool, LANES, page_rows)
    cache_sel4 = jnp.transpose(cache_nsa_sel, (0, 1, 3, 4, 2)).reshape(depth, n_pool, LANES, page_rows)
    win_state4 = jnp.transpose(state_nsa_win, (0, 1, 3, 4, 2)).reshape(depth, n_seq, LANES, w_buf)
    cache_mem4 = jnp.transpose(cache_mem, (0, 1, 3, 4, 5, 2)).reshape(depth, n_seq, 2 * MEM_HEADS * MEM_HD, mem_len)
    mem_flat = mem_prompt.reshape(bsz * mem_len, d)

    x = jnp.concatenate([x_prompt.reshape(tp, d), x_sample.reshape(ts_, d)], axis=0)
    outs = {k: [] for k in ("p_mla", "p_cmp", "p_sel", "p_win", "p_mem", "s_mla", "s_cmp", "s_sel", "s_win")}
    row = lambda a, l: a[l][None, :]
    for l in range(depth):
        w = dict(g_mix=row(g_mix, l), w_in=pw["w_in"][l], g_q_a=row(g_q_a, l), w_q_b=pw["w_q_b"][l],
                 g_kv_a=row(g_kv_a, l), wk_pad=pw["wk_pad"][l], wv_pair=pw["wv_pair"][l], g_mla_q=row(gq_pad, l),
                 g_mla_k=row(gk_pad, l), g_nsa_q=row(two(g_nsa_q), l), g_sel_k=row(two(g_sel_k), l),
                 g_win_k=row(two(g_win_k), l), b_gate=row(bg_pad, l), wk_c=pw["wk_c"][l], wk_ct=pw["wk_ct"][l],
                 wv_c=pw["wv_c"][l], g_kn=row(gn_tiled, l), g_kr=row(g_mla_k[:, MLA_NOPE:], l))
        qm, lat, km, vm, qn, cmp_r, sel_r, win_r, selkv, winkv, gates = _proj_call(x, w, tabs, tm)

        p3 = lambda a: a[:tp].reshape(bsz, seq, a.shape[-1])
        o_m_p = _mla_attn_call(p3(qm), p3(km), p3(vm), tq_mla)
        g_cmp2 = row(two(g_cmp_k), l)
        cmpx_p = _cmp_pool_prompt_call(p3(cmp_r), pw["w_pool"][l], g_cmp2, ce_p, se_p)
        o_n_p = _nsa_prompt_call(p3(qn), p3(gates), cmpx_p, p3(selkv), p3(winkv), consts["e_prompt"], tq_nsa, tk_nsa)

        s3 = lambda a: a[tp:].reshape(n_seq, n_tok, a.shape[-1])
        o_m_s = _mla_decode_call(page_table, s3(qm), s3(lat), cache_mla4, l, dtabs, w, consts, pps)
        cmpx_s = _cmp_pool_sample_call(page_table, cache_cmp4, l, pw["w_pool"][l], g_cmp2, ce_s, se_s, pps)
        o_n_s = _nsa_decode_call(page_table, s3(qn), s3(gates), cmpx_s, win_state4, l, s3(win_r), s3(sel_r),
                                 cache_sel4, consts, pps, past_len)

        o_m = jnp.concatenate([o_m_p.reshape(tp, -1), o_m_s.reshape(ts_, -1)], axis=0)
        o_n = jnp.concatenate([o_n_p.reshape(tp, -1), o_n_s.reshape(ts_, -1)], axis=0)
        nm = MLA_HEADS * MLA_V
        x = _mm_res_call(x, [(o_m, pw["w_out"][l][:nm]), (o_n, pw["w_out"][l][nm:])], tm)

        mkv = _memkv_call(mem_flat, row(g_mem_ctx, l), pw["w_mem_kv"][l], row(two(g_mem_k), l))
        q_mem = _norm_mm_call(x, row(g_mem, l), pw["w_mem_q"][l], tm)
        gq2 = row(two(g_mem_q), l)
        om_p = _mem_attn_call(q_mem[:tp].reshape(bsz, seq, -1), mkv.reshape(bsz, mem_len, -1), gq2,
                              _pick(seq, (512, 256, 128)), False)
        om_s = _mem_attn_call(q_mem[tp:].reshape(n_seq, n_tok, -1), cache_mem4[l], gq2, n_tok, True)
        o_mem = jnp.concatenate([om_p.reshape(tp, -1), om_s.reshape(ts_, -1)], axis=0)
        x = _mm_res_call(x, [(o_mem, pw["w_mem_o"][l])], tm)

        x = _mlp_call(x, row(g_mlp, l), pw["w_up"][l], pw["w_down"][l], tm_mlp)

        outs["p_mla"].append(lat[:tp].reshape(bsz, seq, -1))
        outs["p_cmp"].append(cmp_r[:tp].reshape(bsz, seq, 2, NSA_HD))
        outs["p_sel"].append(sel_r[:tp].reshape(bsz, seq, 2, NSA_HD))
        outs["p_win"].append(win_r[:tp].reshape(bsz, seq, 2, NSA_HD)[:, seq - n_win:])
        outs["p_mem"].append(mkv.reshape(bsz, mem_len, 2, MEM_HEADS, MEM_HD))
        outs["s_mla"].append(lat[tp:].reshape(n_seq, n_tok, -1))
        outs["s_cmp"].append(cmp_r[tp:].reshape(n_seq, n_tok, 2, NSA_HD))
        outs["s_sel"].append(sel_r[tp:].reshape(n_seq, n_tok, 2, NSA_HD))
        win_all = jnp.concatenate([state_nsa_win[l], win_r[tp:].reshape(n_seq, n_tok, 2, NSA_HD)], axis=1)
        outs["s_win"].append(win_all[:, win_all.shape[1] - w_buf:])

    st = lambda k: jnp.stack(outs[k])
    return (x[:tp].reshape(bsz, seq, d), x[tp:].reshape(n_seq, n_tok, d), st("p_mla"), st("p_cmp"), st("p_sel"),
            st("p_win"), st("p_mem"), st("s_mla"), st("s_cmp"), st("s_sel"), st("s_win"))
```

```python
import functools

import numpy as np
import jax
import jax.numpy as jnp
from jax import lax
from jax.experimental import pallas as pl
from jax.experimental.pallas import tpu as pltpu

F32 = jnp.float32
BF16 = jnp.bfloat16

MLA_HEADS = 8
MLA_NOPE = 64
MLA_ROPE = 32
MLA_V = 64
MLA_QK = MLA_NOPE + MLA_ROPE
NSA_HEADS = 8
NSA_HD = 64
CMP_BLOCK = 64
SEL_TOPK = 16
WINDOW = 512
MEM_HEADS = 4
MEM_HD = 64
ROPE_THETA = 10000.0
EPS = 1e-6
LANES = 128
NEG = -1e30
M_INIT = -1e20
VMEM_LIMIT = 56 * 1024 * 1024

SEG_CQ, SEG_CKV, SEG_KR, SEG_QN, SEG_CMP, SEG_SEL, SEG_WIN, SEG_GL, IN_PAD = 0, 256, 384, 512, 1024, 1152, 1280, 1408, 1536


def _dot(a, b):
    return jnp.dot(a, b, preferred_element_type=F32)


def _dot_nt(a, b):
    return lax.dot_general(a, b, (((1,), (1,)), ((), ())), preferred_element_type=F32)


def _dot_tn(a, b):
    return lax.dot_general(a, b, (((0,), (0,)), ((), ())), preferred_element_type=F32)


def _pick(n, cands):
    for c in cands:
        if n % c == 0:
            return c
    raise ValueError(f"no tile in {cands} divides {n}")


def _params(sem):
    return pltpu.CompilerParams(dimension_semantics=sem, vmem_limit_bytes=VMEM_LIMIT)


def _lane(shape):
    return lax.broadcasted_iota(jnp.int32, shape, len(shape) - 1)


def _rms_full(x, g, n):
    ms = jnp.sum(x * x, axis=-1, keepdims=True) * (1.0 / n)
    return x * lax.rsqrt(ms + EPS) * g


def _swap_half(x, half):
    lane = _lane(x.shape)
    up = pltpu.roll(x, LANES - half, axis=1)
    dn = pltpu.roll(x, half, axis=1)
    return jnp.where((lane % (2 * half)) < half, up, dn)


def _kv2(x):
    lo = _lane(x.shape) < 64
    sw = pltpu.roll(x, 64, axis=1)
    return jnp.concatenate([jnp.where(lo, x, sw), jnp.where(lo, sw, x)], axis=1).astype(BF16)


def _norm_rope_k64(x, g, c, s):
    lo = _lane(x.shape) < 64
    ssq = jnp.sum(jnp.where(lo, x * x, 0.0), axis=-1, keepdims=True)
    kx = x * lax.rsqrt(ssq * (1.0 / 64) + EPS) * g
    roped = kx * c + _swap_half(kx, 32) * s
    return jnp.where(lo, roped, x)


def _rows_from_t(xt):
    feat, npos = xt.shape
    if feat < LANES:
        xt = jnp.concatenate([xt, jnp.zeros((LANES - feat, npos), xt.dtype)], axis=0)
    blocks = [xt[:, LANES * k:LANES * (k + 1)].T for k in range(npos // LANES)]
    return blocks[0] if len(blocks) == 1 else jnp.concatenate(blocks, axis=0)


def _split3(p):
    hi = p.astype(BF16)
    r1 = p - hi.astype(F32)
    mid = r1.astype(BF16)
    lo = (r1 - mid.astype(F32)).astype(BF16)
    return hi, mid, lo


def _proj_kernel(x_ref, gmix_ref, win_ref, gqa_ref, wqb_ref, gkva_ref, wk_ref, wv_ref, gq_ref, gk_ref,
                 gnq_ref, gsel_ref, gwin_ref, bg_ref, c64_ref, s64_ref, cm_ref, sm_ref,
                 qm_ref, lat_ref, km_ref, vm_ref, qn_ref, cmp_ref, sel_ref, wino_ref, selkv_ref, winkv_ref,
                 gate_ref):
    x = x_ref[...]
    d = x.shape[-1]
    h = _rms_full(x, gmix_ref[...], d)
    z = _dot(h.astype(BF16), win_ref[...])
    c64, s64, cm, sm = c64_ref[...], s64_ref[...], cm_ref[...], sm_ref[...]
    lane = _lane(c64.shape)
    lo = lane < 64

    cqn = _rms_full(z[:, SEG_CQ:SEG_CKV], gqa_ref[...], SEG_CKV - SEG_CQ)
    qm = _dot(cqn.astype(BF16), wqb_ref[...])
    gq = gq_ref[...]
    for hd in range(MLA_HEADS):
        xh = qm[:, LANES * hd:LANES * (hd + 1)]
        ssq = jnp.sum(xh * xh, axis=-1, keepdims=True)
        xh = xh * lax.rsqrt(ssq * (1.0 / MLA_QK) + EPS) * gq
        xh = (xh * cm + _swap_half(xh, MLA_ROPE // 2) * sm) * (MLA_QK ** -0.5)
        qm_ref[:, LANES * hd:LANES * (hd + 1)] = xh.astype(BF16)

    ckvn = _rms_full(z[:, SEG_CKV:SEG_KR], gkva_ref[...], SEG_KR - SEG_CKV)
    krp = z[:, SEG_KR:SEG_QN]
    lat_ref[:, 0:128] = ckvn
    lat_ref[:, 128:160] = krp[:, 0:MLA_ROPE]
    ckvb = ckvn.astype(BF16)
    kn = _dot(ckvb, wk_ref[...])
    vm_ref[...] = _dot(ckvb, wv_ref[...]).astype(BF16)
    krpad = jnp.where((lane >= 64) & (lane < 96), krp, 0.0)
    gk = gk_ref[...]
    for hd in range(MLA_HEADS):
        kh = kn[:, LANES * hd:LANES * (hd + 1)] + krpad
        ssq = jnp.sum(kh * kh, axis=-1, keepdims=True)
        kh = kh * lax.rsqrt(ssq * (1.0 / MLA_QK) + EPS) * gk
        kh = kh * cm + _swap_half(kh, MLA_ROPE // 2) * sm
        km_ref[:, LANES * hd:LANES * (hd + 1)] = kh.astype(BF16)

    gnq = gnq_ref[...]
    for ch in range(NSA_HEADS // 2):
        xc = z[:, SEG_QN + LANES * ch:SEG_QN + LANES * (ch + 1)]
        sq = xc * xc
        s_lo = jnp.sum(jnp.where(lo, sq, 0.0), axis=-1, keepdims=True)
        s_hi = jnp.sum(jnp.where(lo, 0.0, sq), axis=-1, keepdims=True)
        r = jnp.where(lo, lax.rsqrt(s_lo * (1.0 / NSA_HD) + EPS), lax.rsqrt(s_hi * (1.0 / NSA_HD) + EPS))
        xc = xc * r * gnq
        xc = (xc * c64 + _swap_half(xc, NSA_HD // 2) * s64) * (NSA_HD ** -0.5)
        qn_ref[:, LANES * ch:LANES * (ch + 1)] = xc.astype(BF16)

    cmp_ref[...] = z[:, SEG_CMP:SEG_SEL]
    sel = _norm_rope_k64(z[:, SEG_SEL:SEG_WIN], gsel_ref[...], c64, s64)
    sel_ref[...] = sel
    selkv_ref[...] = _kv2(sel)
    win = _norm_rope_k64(z[:, SEG_WIN:SEG_GL], gwin_ref[...], c64, s64)
    wino_ref[...] = win
    winkv_ref[...] = _kv2(win)
    gate_ref[...] = 1.0 / (1.0 + jnp.exp(-(z[:, SEG_GL:IN_PAD] + bg_ref[...])))


def _proj_call(x, w, tabs, tm):
    t, d = x.shape
    row = lambda n: pl.BlockSpec((tm, n), lambda i: (i, 0))
    full = lambda a: pl.BlockSpec(a.shape, lambda i: (0,) * a.ndim)
    ins = [x, w["g_mix"], w["w_in"], w["g_q_a"], w["w_q_b"], w["g_kv_a"], w["wk_pad"], w["wv_pair"], w["g_mla_q"],
           w["g_mla_k"], w["g_nsa_q"], w["g_sel_k"], w["g_win_k"], w["b_gate"]]
    in_specs = [row(d)] + [full(a) for a in ins[1:]] + [row(LANES)] * 4
    outs = [(1024, BF16), (160, F32), (1024, BF16), (1024, BF16), (512, BF16), (128, F32), (128, F32), (128, F32),
            (256, BF16), (256, BF16), (128, F32)]
    return pl.pallas_call(
        _proj_kernel,
        grid=(t // tm,),
        in_specs=in_specs,
        out_specs=[row(n) for n, _ in outs],
        out_shape=[jax.ShapeDtypeStruct((t, n), dt) for n, dt in outs],
        compiler_params=_params(("parallel",)),
    )(*ins, tabs["c64"], tabs["s64"], tabs["cm"], tabs["sm"])


def _mm_res_kernel(*refs, n):
    res_ref, out_ref = refs[0], refs[-1]
    acc = res_ref[...]
    for k in range(n):
        acc = acc + _dot(refs[1 + 2 * k][...], refs[2 + 2 * k][...])
    out_ref[...] = acc


def _mm_res_call(res, pairs, tm):
    t, d = res.shape
    row = lambda n: pl.BlockSpec((tm, n), lambda i: (i, 0))
    full = lambda a: pl.BlockSpec(a.shape, lambda i: (0,) * a.ndim)
    ins, specs = [res], [row(d)]
    for a, wgt in pairs:
        ins += [a, wgt]
        specs += [row(a.shape[1]), full(wgt)]
    return pl.pallas_call(
        functools.partial(_mm_res_kernel, n=len(pairs)),
        grid=(t // tm,), in_specs=specs, out_specs=row(d),
        out_shape=jax.ShapeDtypeStruct((t, d), F32),
        compiler_params=_params(("parallel",)),
    )(*ins)


def _norm_mm_kernel(x_ref, g_ref, w_ref, o_ref):
    x = x_ref[...]
    o_ref[...] = _dot(_rms_full(x, g_ref[...], x.shape[-1]).astype(BF16), w_ref[...])


def _norm_mm_call(x, g, wgt, tm):
    t, d = x.shape
    n = wgt.shape[1]
    return pl.pallas_call(
        _norm_mm_kernel,
        grid=(t // tm,),
        in_specs=[pl.BlockSpec((tm, d), lambda i: (i, 0)), pl.BlockSpec(g.shape, lambda i: (0, 0)),
                  pl.BlockSpec(wgt.shape, lambda i: (0, 0))],
        out_specs=pl.BlockSpec((tm, n), lambda i: (i, 0)),
        out_shape=jax.ShapeDtypeStruct((t, n), F32),
        compiler_params=_params(("parallel",)),
    )(x, g, wgt)


def _head_rms64(x, g2):
    lo = _lane(x.shape) < 64
    sq = x * x
    s_lo = jnp.sum(jnp.where(lo, sq, 0.0), axis=-1, keepdims=True)
    s_hi = jnp.sum(jnp.where(lo, 0.0, sq), axis=-1, keepdims=True)
    r = jnp.where(lo, lax.rsqrt(s_lo * (1.0 / 64) + EPS), lax.rsqrt(s_hi * (1.0 / 64) + EPS))
    return x * r * g2


def _memkv_kernel(x_ref, g_ref, w_ref, gk_ref, o_ref):
    x = x_ref[...]
    z = _dot(_rms_full(x, g_ref[...], x.shape[-1]).astype(BF16), w_ref[...])
    gk = gk_ref[...]
    kw = MEM_HEADS * MEM_HD
    for ch in range(kw // LANES):
        o_ref[:, LANES * ch:LANES * (ch + 1)] = _head_rms64(z[:, LANES * ch:LANES * (ch + 1)], gk)
    o_ref[:, kw:2 * kw] = z[:, kw:2 * kw]


def _memkv_call(x, g, wkv, gk2):
    t, d = x.shape
    tm = _pick(t, (256, 128))
    n = wkv.shape[1]
    return pl.pallas_call(
        _memkv_kernel,
        grid=(t // tm,),
        in_specs=[pl.BlockSpec((tm, d), lambda i: (i, 0)), pl.BlockSpec(g.shape, lambda i: (0, 0)),
                  pl.BlockSpec(wkv.shape, lambda i: (0, 0)), pl.BlockSpec(gk2.shape, lambda i: (0, 0))],
        out_specs=pl.BlockSpec((tm, n), lambda i: (i, 0)),
        out_shape=jax.ShapeDtypeStruct((t, n), F32),
        compiler_params=_params(("parallel",)),
    )(x, g, wkv, gk2)


def _mem_attn_one(q, kv_ref, gq, kv_feature_major):
    tq = q.shape[0]
    kw = MEM_HEADS * MEM_HD
    lo = _lane((tq, LANES)) < 64
    outs = []
    for pr in range(kw // LANES):
        qp = _head_rms64(q[:, LANES * pr:LANES * (pr + 1)], gq) * (MEM_HD ** -0.5)
        q2 = jnp.concatenate([jnp.where(lo, qp, 0.0), jnp.where(lo, 0.0, qp)], axis=0).astype(BF16)
        if kv_feature_major:
            s = _dot(q2, kv_ref[LANES * pr:LANES * (pr + 1), :].astype(BF16))
        else:
            s = _dot_nt(q2, kv_ref[:, LANES * pr:LANES * (pr + 1)].astype(BF16))
        m = jnp.max(s, axis=-1, keepdims=True)
        p = jnp.exp(s - m)
        p = (p / jnp.maximum(jnp.sum(p, axis=-1, keepdims=True), 1e-30)).astype(BF16)
        if kv_feature_major:
            o = _dot_nt(p, kv_ref[kw + LANES * pr:kw + LANES * (pr + 1), :].astype(BF16))
        else:
            o = _dot(p, kv_ref[:, kw + LANES * pr:kw + LANES * (pr + 1)].astype(BF16))
        outs.append(jnp.where(lo, o[:tq], o[tq:]).astype(BF16))
    return outs


def _mem_attn_prompt_kernel(q_ref, kv_ref, gq_ref, o_ref):
    outs = _mem_attn_one(q_ref[...], kv_ref.at[0], gq_ref[...], False)
    for pr, o in enumerate(outs):
        o_ref[:, LANES * pr:LANES * (pr + 1)] = o


def _mem_attn_sample_kernel(q_ref, kv_ref, gq_ref, prev_ref, o_ref, *, group, n_tok):
    del prev_ref
    for gi in range(group):
        outs = _mem_attn_one(q_ref[gi * n_tok:(gi + 1) * n_tok, :], kv_ref.at[0, gi], gq_ref[...], True)
        for pr, o in enumerate(outs):
            o_ref[gi * n_tok:(gi + 1) * n_tok, LANES * pr:LANES * (pr + 1)] = o


def _mem_attn_prompt_call(q, kv, gq2, bsz, seq, tq):
    t, kw = q.shape
    nq = seq // tq
    return pl.pallas_call(
        _mem_attn_prompt_kernel,
        grid=(bsz, nq),
        in_specs=[pl.BlockSpec((tq, kw), lambda b, i: (b * nq + i, 0)),
                  pl.BlockSpec((1,) + kv.shape[1:], lambda b, i: (b, 0, 0)),
                  pl.BlockSpec(gq2.shape, lambda b, i: (0, 0))],
        out_specs=pl.BlockSpec((tq, kw), lambda b, i: (b * nq + i, 0)),
        out_shape=jax.ShapeDtypeStruct((t, kw), BF16),
        compiler_params=_params(("parallel", "parallel")),
    )(q, kv, gq2)


def _mem_attn_sample_call(q, kv_t, layer, gq2, prev, row0, n_seq, n_tok, group):
    t, kw = q.shape
    rows = group * n_tok
    blk0 = row0 // rows
    return pl.pallas_call(
        functools.partial(_mem_attn_sample_kernel, group=group, n_tok=n_tok),
        grid=(n_seq // group,),
        in_specs=[pl.BlockSpec((rows, kw), lambda s: (blk0 + s, 0)),
                  pl.BlockSpec((1, group) + kv_t.shape[2:], lambda s: (layer, s, 0, 0)),
                  pl.BlockSpec(gq2.shape, lambda s: (0, 0)),
                  pl.BlockSpec(memory_space=pl.ANY)],
        out_specs=pl.BlockSpec((rows, kw), lambda s: (blk0 + s, 0)),
        out_shape=jax.ShapeDtypeStruct((t, kw), BF16),
        input_output_aliases={3: 0},
        compiler_params=_params(("parallel",)),
    )(q, kv_t, gq2, prev)


def _mlp_kernel(x_ref, g_ref, wu_ref, wd_ref, o_ref, h_ref, acc_ref):
    f = pl.program_id(1)

    @pl.when(f == 0)
    def _():
        x = x_ref[...]
        h_ref[...] = _rms_full(x, g_ref[...], x.shape[-1]).astype(BF16)
        acc_ref[...] = x

    u = jnp.maximum(_dot(h_ref[...], wu_ref[...]), 0.0)
    acc_ref[...] += _dot((u * u).astype(BF16), wd_ref[...])

    @pl.when(f == pl.num_programs(1) - 1)
    def _():
        o_ref[...] = acc_ref[...]


def _mlp_call(x, g, wu, wd, tm):
    t, d = x.shape
    ff = wu.shape[1]
    tf = _pick(ff, (1024, 512, 256, 128))
    return pl.pallas_call(
        _mlp_kernel,
        grid=(t // tm, ff // tf),
        in_specs=[pl.BlockSpec((tm, d), lambda i, f: (i, 0)), pl.BlockSpec(g.shape, lambda i, f: (0, 0)),
                  pl.BlockSpec((d, tf), lambda i, f: (0, f)), pl.BlockSpec((tf, d), lambda i, f: (f, 0))],
        out_specs=pl.BlockSpec((tm, d), lambda i, f: (i, 0)),
        out_shape=jax.ShapeDtypeStruct((t, d), F32),
        scratch_shapes=[pltpu.VMEM((tm, d), BF16), pltpu.VMEM((tm, d), F32)],
        compiler_params=_params(("parallel", "arbitrary")),
    )(x, g, wu, wd)


def _mla_attn_kernel(q_ref, k_ref, v_ref, o_ref, *, tq):
    i = pl.program_id(2)
    row = lax.broadcasted_iota(jnp.int32, (tq, tq), 0)
    col = lax.broadcasted_iota(jnp.int32, (tq, tq), 1)
    qs = [q_ref[:, LANES * hh:LANES * (hh + 1)] for hh in range(2)]

    def step(j, carry, masked):
        start = pl.multiple_of(j * tq, tq)
        new = []
        for hh in range(2):
            m, l, acc = carry[hh]
            k = k_ref[pl.ds(start, tq), LANES * hh:LANES * (hh + 1)]
            v = v_ref[pl.ds(start, tq), LANES * hh:LANES * (hh + 1)]
            s = _dot_nt(qs[hh], k)
            if masked:
                s = jnp.where(col <= row, s, NEG)
            m_new = jnp.maximum(m, jnp.max(s, axis=-1, keepdims=True))
            p = jnp.exp(s - m_new)
            alpha = jnp.exp(m - m_new)
            l = alpha * l + jnp.sum(p, axis=-1, keepdims=True)
            acc = alpha * acc + _dot(p.astype(BF16), v)
            new.append((m_new, l, acc))
        return tuple(new)

    one = (jnp.full((tq, 1), M_INIT, F32), jnp.zeros((tq, 1), F32), jnp.zeros((tq, LANES), F32))
    carry = lax.fori_loop(0, i, functools.partial(step, masked=False), (one, one))
    (_, l0, acc0), (_, l1, acc1) = step(i, carry, True)
    o_ref[...] = (acc0 / l0 + acc1 / l1).astype(BF16)


def _mla_attn_call(qm, km, vm, bsz, seq, tq):
    t = qm.shape[0]
    npair = MLA_HEADS // 2
    nq = seq // tq
    return pl.pallas_call(
        functools.partial(_mla_attn_kernel, tq=tq),
        grid=(bsz, npair, nq),
        in_specs=[pl.BlockSpec((tq, 2 * LANES), lambda bb, p, i: (bb * nq + i, p)),
                  pl.BlockSpec((seq, 2 * LANES), lambda bb, p, i: (bb, p)),
                  pl.BlockSpec((seq, 2 * LANES), lambda bb, p, i: (bb, p))],
        out_specs=pl.BlockSpec((tq, LANES), lambda bb, p, i: (bb * nq + i, p)),
        out_shape=jax.ShapeDtypeStruct((t, npair * LANES), BF16),
        compiler_params=_params(("parallel", "parallel", "arbitrary")),
    )(qm, km, vm)


def _pool_blocks(x, wp, g, c, s):
    nb = x.shape[0] // CMP_BLOCK
    pooled = jnp.sum(x.reshape(nb, CMP_BLOCK, LANES) * wp[None], axis=1)
    return _norm_rope_k64(pooled, g, c, s)


def _cmp_pool_kernel(x_ref, wp_ref, g_ref, c_ref, s_ref, o_ref):
    o_ref[0] = _pool_blocks(x_ref[...], wp_ref[...], g_ref[...], c_ref[...], s_ref[...])


def _cmp_pool_prompt_call(cmp_rows, wp, g, c_end, s_end, bsz, seq):
    nb = seq // CMP_BLOCK
    full = lambda a: pl.BlockSpec(a.shape, lambda i: (0,) * a.ndim)
    return pl.pallas_call(
        _cmp_pool_kernel,
        grid=(bsz,),
        in_specs=[pl.BlockSpec((seq, LANES), lambda i: (i, 0)), full(wp), full(g), full(c_end), full(s_end)],
        out_specs=pl.BlockSpec((1, nb, LANES), lambda i: (i, 0, 0)),
        out_shape=jax.ShapeDtypeStruct((bsz, nb, LANES), F32),
        compiler_params=_params(("parallel",)),
    )(cmp_rows, wp, g, c_end, s_end)


def _page_specs(layer, pages_per_step, feat, page_rows, chunk_of=lambda c: c):
    return [pl.BlockSpec((1, 1, feat, page_rows),
                         lambda s, c, pt, i=i: (layer, pt[s, chunk_of(c) * pages_per_step + i], 0, 0))
            for i in range(pages_per_step)]


def _softmax_rows(s, mask):
    s = jnp.where(mask, s, NEG)
    m = jnp.max(s, axis=-1, keepdims=True)
    p = jnp.where(mask, jnp.exp(s - m), 0.0)
    return p / jnp.maximum(jnp.sum(p, axis=-1, keepdims=True), 1e-30)


def _topk_mask(imp, elig, k, axis):
    n = imp.shape[axis]
    idx = lax.broadcasted_iota(jnp.int32, imp.shape, axis)
    work = jnp.where(elig, imp, -jnp.inf)
    sel = jnp.zeros(imp.shape, F32)
    for _ in range(k):
        mx = jnp.max(work, axis=axis, keepdims=True)
        is_max = (work == mx) & (mx > -jnp.inf)
        first = jnp.min(jnp.where(is_max, idx, n), axis=axis, keepdims=True)
        pick = idx == first
        sel = jnp.where(pick, 1.0, sel)
        work = jnp.where(pick, -jnp.inf, work)
    return sel


def _nsa_prompt_kernel(q_ref, gate_ref, cmp_ref, selkv_ref, winkv_ref, e_ref, o_ref, *, tq, tk):
    i = pl.program_id(1)
    s0 = i * tq
    nh = NSA_HEADS
    nb = cmp_ref.shape[1]
    lo = _lane((tq, LANES)) < 64
    q = q_ref[...]
    parts = []
    for pr in range(nh // 2):
        qp = q[:, LANES * pr:LANES * (pr + 1)]
        parts += [jnp.where(lo, qp, 0), jnp.where(lo, 0, qp)]
    q8 = jnp.concatenate(parts, axis=0).astype(BF16)
    qpos = s0 + lax.broadcasted_iota(jnp.int32, (tq, 1), 0)

    ckv = _kv2(cmp_ref[0])
    blk_w = lax.broadcasted_iota(jnp.int32, (nb, nh * tq), 0)
    qpos_w = s0 + lax.broadcasted_iota(jnp.int32, (nb, nh * tq), 1) % tq
    p_ct = _softmax_cols(_dot_nt(ckv[:, 0:LANES], q8), (blk_w * CMP_BLOCK + (CMP_BLOCK - 1)) <= qpos_w)
    o_c = _dot_tn(p_ct.astype(BF16), ckv[:, LANES:2 * LANES])
    imp_t = p_ct[:, 0:tq]
    for hd in range(1, nh):
        imp_t = imp_t + p_ct[:, hd * tq:(hd + 1) * tq]

    blk_t = lax.broadcasted_iota(jnp.int32, (nb, tq), 0)
    cur_t = (s0 + lax.broadcasted_iota(jnp.int32, (nb, tq), 1)) // CMP_BLOCK
    sel_t = jnp.maximum(_topk_mask(imp_t, blk_t < cur_t, min(SEL_TOPK - 1, nb), 0), (blk_t == cur_t).astype(F32))
    selb_t = sel_t.astype(BF16)

    def sel_step(j, carry):
        m, l, acc = carry
        start = pl.multiple_of(j * tk, tk)
        k2 = selkv_ref[pl.ds(start, tk), 0:LANES]
        v2 = selkv_ref[pl.ds(start, tk), LANES:2 * LANES]
        allow = _dot_tn(selb_t, e_ref[j])
        kpos = start + lax.broadcasted_iota(jnp.int32, (1, tk), 1)
        bias = jnp.where((allow > 0.5) & (kpos <= qpos), 0.0, NEG)
        s = _dot_nt(q8, k2).reshape(nh, tq, tk) + bias[None]
        m_new = jnp.maximum(m, jnp.max(s, axis=-1, keepdims=True))
        p = jnp.exp(s - m_new)
        alpha = jnp.exp(m - m_new)
        l = alpha * l + jnp.sum(p, axis=-1, keepdims=True)
        pv = _dot(p.reshape(nh * tq, tk).astype(BF16), v2).reshape(nh, tq, LANES)
        return m_new, l, alpha * acc + pv

    init = (jnp.full((nh, tq, 1), M_INIT, F32), jnp.zeros((nh, tq, 1), F32), jnp.zeros((nh, tq, LANES), F32))
    n_tiles = (s0 + tq + tk - 1) // tk
    _, l_s, acc_s = lax.fori_loop(0, n_tiles, sel_step, init)
    o_s = (acc_s / l_s).reshape(nh * tq, LANES)

    wlen = WINDOW + tq
    wstart = pl.multiple_of(jnp.maximum(s0 - WINDOW, 0), tq)
    wk = winkv_ref[pl.ds(wstart, wlen), 0:LANES]
    wv = winkv_ref[pl.ds(wstart, wlen), LANES:2 * LANES]
    wpos = wstart + lax.broadcasted_iota(jnp.int32, (1, wlen), 1)
    wmask = (wpos <= qpos) & (wpos > qpos - WINDOW)
    p_w = _softmax_rows(_dot_nt(q8, wk).reshape(nh, tq, wlen), wmask[None])
    o_w = _dot(p_w.reshape(nh * tq, wlen).astype(BF16), wv)

    gate = gate_ref[...]
    gcols = [[gate[:, 3 * hd + c:3 * hd + c + 1] for hd in range(nh)] for c in range(3)]
    g_c, g_s, g_w = [jnp.concatenate(cols, axis=0) for cols in gcols]
    mixed = g_c * o_c + g_s * o_s + g_w * o_w
    for pr in range(nh // 2):
        ev = mixed[(2 * pr) * tq:(2 * pr + 1) * tq]
        od = mixed[(2 * pr + 1) * tq:(2 * pr + 2) * tq]
        o_ref[:, LANES * pr:LANES * (pr + 1)] = jnp.where(lo, ev, od).astype(BF16)


def _nsa_prompt_call(qn, gates, cmpx, selkv, winkv, e_exp, bsz, seq, tq, tk):
    t = qn.shape[0]
    nb = cmpx.shape[1]
    nq = seq // tq
    return pl.pallas_call(
        functools.partial(_nsa_prompt_kernel, tq=tq, tk=tk),
        grid=(bsz, nq),
        in_specs=[pl.BlockSpec((tq, 512), lambda bb, i: (bb * nq + i, 0)),
                  pl.BlockSpec((tq, LANES), lambda bb, i: (bb * nq + i, 0)),
                  pl.BlockSpec((1, nb, LANES), lambda bb, i: (bb, 0, 0)),
                  pl.BlockSpec((seq, 2 * LANES), lambda bb, i: (bb, 0)),
                  pl.BlockSpec((seq, 2 * LANES), lambda bb, i: (bb, 0)),
                  pl.BlockSpec(e_exp.shape, lambda bb, i: (0, 0, 0))],
        out_specs=pl.BlockSpec((tq, 512), lambda bb, i: (bb * nq + i, 0)),
        out_shape=jax.ShapeDtypeStruct((t, 512), BF16),
        compiler_params=_params(("parallel", "arbitrary")),
    )(qn, gates, cmpx, selkv, winkv, e_exp)


def _online_update(s, val_b, m_ref, l_ref, acc_ref):
    m_old = m_ref[...]
    m_new = jnp.maximum(m_old, jnp.max(s, axis=0, keepdims=True))
    p = jnp.exp(s - m_new)
    alpha = jnp.exp(m_old - m_new)
    l_ref[...] = alpha * l_ref[...] + jnp.sum(p, axis=0, keepdims=True)
    acc_ref[...] = alpha * acc_ref[...] + _dot_tn(val_b, p.astype(BF16))
    m_ref[...] = m_new


def _softmax_cols(s, mask):
    s = jnp.where(mask, s, NEG)
    m = jnp.max(s, axis=0, keepdims=True)
    p = jnp.where(mask, jnp.exp(s - m), 0.0)
    return p / jnp.maximum(jnp.sum(p, axis=0, keepdims=True), 1e-30)


def _fold_heads(y, n_tok):
    rows, width = y.shape
    r = lax.broadcasted_iota(jnp.int32, y.shape, 0)
    c = lax.broadcasted_iota(jnp.int32, y.shape, 1)
    y = jnp.where((c // 64) == (r // n_tok), y, 0.0)
    return jnp.sum(y.reshape(rows // n_tok, n_tok, width), axis=0)


def _head_rows(q, width):
    n_tok, tot = q.shape
    nh = tot // width
    rep = jnp.concatenate([q] * nh, axis=0)
    r = lax.broadcasted_iota(jnp.int32, rep.shape, 0)
    c = lax.broadcasted_iota(jnp.int32, rep.shape, 1)
    return jnp.where((c // width) == (r // n_tok), rep, jnp.zeros_like(rep))


def _mla_decode_kernel(*refs, pps):
    refs = refs[1:]
    q_ref, latnew_ref = refs[0], refs[1]
    page_refs = refs[2:2 + pps]
    (tc_ref, ts_ref, tcn_ref, tsn_ref, wk_ref, wkt_ref, wv_ref, ones_ref, seln_ref, selr_ref, selb_ref, gn_ref,
     gr_ref, _prev_ref, o_ref, a1_ref, a2_ref, b2_ref, m_ref, l_ref, acc_ref) = refs[2 + pps:]
    c = pl.program_id(1)
    n_tok = q_ref.shape[0]

    @pl.when(c == 0)
    def _():
        qbd = _head_rows(q_ref[...], LANES)
        qn = _dot(qbd, seln_ref[...]) * gn_ref[...]
        a1_ref[...] = _dot(qn.astype(BF16), wkt_ref[...]).astype(BF16)
        a2_ref[...] = (_dot(qbd, selr_ref[...]) * gr_ref[...]).astype(BF16)
        b2_ref[...] = (_dot(qbd, selb_ref[...]) * gr_ref[...]).astype(BF16)
        m_ref[...] = jnp.full(m_ref.shape, M_INIT, F32)
        l_ref[...] = jnp.zeros(l_ref.shape, F32)
        acc_ref[...] = jnp.zeros(acc_ref.shape, F32)

    def chunk(ckv, kr, c2, s2, mask):
        ckvb = ckv.astype(BF16)
        kn = _dot(ckvb, wk_ref[...])
        ssq = _dot((kn * kn).astype(BF16), ones_ref[...])
        krsq = jnp.sum(kr * kr, axis=-1, keepdims=True)
        r = lax.rsqrt((ssq + krsq) * (1.0 / MLA_QK) + EPS)
        s = (_dot_nt(ckvb, a1_ref[...]) + _dot_nt((kr * c2).astype(BF16), a2_ref[...])
             + _dot_nt((kr * s2).astype(BF16), b2_ref[...])) * r
        if mask is not None:
            s = jnp.where(mask, s, NEG)
        _online_update(s, ckvb, m_ref, l_ref, acc_ref)

    ckv = jnp.concatenate([_rows_from_t(pr[0, 0, 0:128, :]) for pr in page_refs], axis=0)
    kr = jnp.concatenate([_rows_from_t(pr[0, 0, 128:128 + MLA_ROPE, :]) for pr in page_refs], axis=0)
    chunk(ckv, kr[:, 0:MLA_ROPE], tc_ref[...], ts_ref[...], None)

    @pl.when(c == pl.num_programs(1) - 1)
    def _():
        rr = lax.broadcasted_iota(jnp.int32, (n_tok, MLA_HEADS * n_tok), 0)
        cc = lax.broadcasted_iota(jnp.int32, (n_tok, MLA_HEADS * n_tok), 1)
        latn = latnew_ref[...]
        chunk(latn[:, 0:128], latn[:, 128:128 + MLA_ROPE], tcn_ref[...], tsn_ref[...], rr <= (cc % n_tok))
        olat = (acc_ref[...] / l_ref[...]).astype(BF16)
        res = _dot_tn(olat, wv_ref[...])
        o_ref[...] = _fold_heads(res, n_tok).astype(BF16)


def _mla_decode_call(page_table, qm, lat, prev, row0, n_tok, cache, layer, tabs, w, consts, pps):
    n_seq, n_pages = page_table.shape
    lw, page_rows = cache.shape[2], cache.shape[3]
    hq = MLA_HEADS * n_tok
    rows = pps * page_rows
    blk0 = row0 // n_tok
    full = lambda a: pl.BlockSpec(a.shape, lambda s, c, pt: (0,) * a.ndim)
    tok = lambda a: pl.BlockSpec((n_tok, a.shape[1]), lambda s, c, pt: (blk0 + s, 0))
    consts_in = [tabs["tcn"], tabs["tsn"], w["wk_c"], w["wk_ct"], w["wv_c"], consts["ones64"], consts["seln"],
                 consts["selr"], consts["selb"], w["g_kn"], w["g_kr"]]
    ins = [page_table, qm, lat] + [cache] * pps + [tabs["tc"], tabs["ts"]] + consts_in + [prev]
    grid_spec = pltpu.PrefetchScalarGridSpec(
        num_scalar_prefetch=1,
        grid=(n_seq, n_pages // pps),
        in_specs=[tok(qm), tok(lat)] + _page_specs(layer, pps, lw, page_rows) + [
            pl.BlockSpec((rows, MLA_ROPE), lambda s, c, pt: (c, 0)),
            pl.BlockSpec((rows, MLA_ROPE), lambda s, c, pt: (c, 0))] + [full(a) for a in consts_in] + [
            pl.BlockSpec(memory_space=pl.ANY)],
        out_specs=pl.BlockSpec((n_tok, MLA_HEADS * MLA_V), lambda s, c, pt: (blk0 + s, 0)),
        scratch_shapes=[pltpu.VMEM((hq, 128), BF16), pltpu.VMEM((hq, MLA_ROPE), BF16), pltpu.VMEM((hq, MLA_ROPE), BF16),
                        pltpu.VMEM((1, hq), F32), pltpu.VMEM((1, hq), F32), pltpu.VMEM((128, hq), F32)],
    )
    return pl.pallas_call(
        functools.partial(_mla_decode_kernel, pps=pps),
        grid_spec=grid_spec,
        out_shape=jax.ShapeDtypeStruct(prev.shape, BF16),
        input_output_aliases={len(ins) - 1: 0},
        compiler_params=_params(("parallel", "arbitrary")),
    )(*ins)


def _nsa_decode_kernel(*refs, pps, past_len):
    refs = refs[1:]
    q_ref, gate_ref, wbuf_ref, winnew_ref, selnew_ref = refs[:5]
    cmp_pages = refs[5:5 + pps]
    sel_pages = refs[5 + pps:5 + 2 * pps]
    (wp_ref, gc_ref, ce_ref, se_ref, e_ref, sel512_ref, hsum_ref, rep_ref, pick_ref, _prev_ref, o_ref, cmpx_ref,
     qrow_ref, selt_ref, oc_ref, ow_ref, m_ref, l_ref, acc_ref) = refs[5 + 2 * pps:]
    c = pl.program_id(1)
    nc = pl.num_programs(1) // 2
    n_tok = q_ref.shape[0]
    hq = NSA_HEADS * n_tok
    nb = cmpx_ref.shape[0]
    bps = ce_ref.shape[0]

    def lane_tok(shape):
        return lax.broadcasted_iota(jnp.int32, shape, 1) % n_tok

    @pl.when(c < nc)
    def _():
        x = jnp.concatenate([_rows_from_t(pr[0, 0]) for pr in cmp_pages], axis=0)
        pooled = _pool_blocks(x, wp_ref[...], gc_ref[...], ce_ref[...], se_ref[...])
        cmpx_ref[pl.ds(pl.multiple_of(c * bps, bps), bps), :] = pooled

    @pl.when(c == nc - 1)
    def _():
        qrows = _dot(_head_rows(q_ref[...], NSA_HD), sel512_ref[...]).astype(BF16)
        qrow_ref[...] = qrows
        xc = cmpx_ref[...].astype(BF16)
        blk = lax.broadcasted_iota(jnp.int32, (nb, hq), 0)
        qpos = past_len + lane_tok((nb, hq))
        p_c = _softmax_cols(_dot_nt(xc, qrows), (blk * CMP_BLOCK + (CMP_BLOCK - 1)) <= qpos)
        oc_ref[...] = _dot_tn(xc, p_c.astype(BF16))[64:128]
        hi, mid, low = _split3(p_c)
        hs = hsum_ref[...]
        imp = _dot(hi, hs) + _dot(mid, hs) + _dot(low, hs)
        selt_ref[...] = _topk_mask(imp, blk < (qpos // CMP_BLOCK), min(SEL_TOPK - 1, nb), 0)
        xw = jnp.concatenate([_rows_from_t(wbuf_ref[0, 0]), winnew_ref[...]], axis=0)
        wl = xw.shape[0]
        wpos = (past_len - wbuf_ref.shape[3]) + lax.broadcasted_iota(jnp.int32, (wl, hq), 0)
        qposw = past_len + lane_tok((wl, hq))
        xwb = xw.astype(BF16)
        p_w = _softmax_cols(_dot_nt(xwb, qrows), (wpos <= qposw) & (wpos > qposw - WINDOW) & (wpos >= 0))
        ow_ref[...] = _dot_tn(xwb, p_w.astype(BF16))[64:128]
        m_ref[...] = jnp.full(m_ref.shape, M_INIT, F32)
        l_ref[...] = jnp.zeros(l_ref.shape, F32)
        acc_ref[...] = jnp.zeros(acc_ref.shape, F32)

    @pl.when(c >= nc)
    def _():
        xs = jnp.concatenate([_rows_from_t(pr[0, 0]) for pr in sel_pages], axis=0).astype(BF16)
        allow = _dot(e_ref[0], selt_ref[...].astype(BF16))
        s = _dot_nt(xs, qrow_ref[...])
        _online_update(jnp.where(allow > 0.5, s, NEG), xs, m_ref, l_ref, acc_ref)

    @pl.when(c == pl.num_programs(1) - 1)
    def _():
        xn = selnew_ref[...].astype(BF16)
        rr = lax.broadcasted_iota(jnp.int32, (n_tok, hq), 0)
        sn = _dot_nt(xn, qrow_ref[...])
        _online_update(jnp.where(rr <= lane_tok((n_tok, hq)), sn, NEG), xn, m_ref, l_ref, acc_ref)
        o_s = acc_ref[...][64:128] / l_ref[...]
        gate = gate_ref[...]
        ghi = gate.astype(BF16)
        glo = (gate - ghi.astype(F32)).astype(BF16)
        diag = lax.broadcasted_iota(jnp.int32, (n_tok, hq), 0) == lane_tok((n_tok, hq))
        gs = []
        for b in range(3):
            rb = _dot(ghi, pick_ref[b]) + _dot(glo, pick_ref[b])
            gs.append(jnp.sum(jnp.where(diag, rb, 0.0), axis=0, keepdims=True))
        mixed = gs[0] * oc_ref[...] + gs[1] * o_s + gs[2] * ow_ref[...]
        ytile = _dot_tn(mixed.astype(BF16), rep_ref[...])
        o_ref[...] = _fold_heads(ytile, n_tok).astype(BF16)


def _nsa_decode_call(page_table, qn, gates, win_rows, sel_rows, prev, row0, n_tok, win_state, cache_cmp, cache_sel,
                     layer, wp, g_cmp, c_end, s_end, consts, pps, past_len):
    n_seq, n_pages = page_table.shape
    page_rows = cache_sel.shape[3]
    hq = NSA_HEADS * n_tok
    nb = past_len // CMP_BLOCK
    rows = pps * page_rows
    bps = rows // CMP_BLOCK
    nc = n_pages // pps
    blk0 = row0 // n_tok
    full = lambda a: pl.BlockSpec(a.shape, lambda s, c, pt: (0,) * a.ndim)
    tok = lambda a: pl.BlockSpec((n_tok, a.shape[1]), lambda s, c, pt: (blk0 + s, 0))
    in_a = lambda c: jnp.minimum(c, nc - 1)
    in_b = lambda c: jnp.maximum(c - nc, 0)
    consts_in = [consts["sel512"], consts["hsum"], consts["rep"], consts["pick"]]
    ins = ([page_table, qn, gates, win_state, win_rows, sel_rows] + [cache_cmp] * pps + [cache_sel] * pps
           + [wp, g_cmp, c_end, s_end, consts["e_dec"]] + consts_in + [prev])
    grid_spec = pltpu.PrefetchScalarGridSpec(
        num_scalar_prefetch=1,
        grid=(n_seq, 2 * nc),
        in_specs=[tok(qn), tok(gates),
                  pl.BlockSpec((1, 1) + win_state.shape[2:], lambda s, c, pt: (layer, s, 0, 0)),
                  tok(win_rows), tok(sel_rows)]
        + _page_specs(layer, pps, LANES, page_rows, in_a) + _page_specs(layer, pps, LANES, page_rows, in_b)
        + [full(wp), full(g_cmp), pl.BlockSpec((bps, LANES), lambda s, c, pt: (in_a(c), 0)),
           pl.BlockSpec((bps, LANES), lambda s, c, pt: (in_a(c), 0)),
           pl.BlockSpec((1, rows, nb), lambda s, c, pt: (in_b(c), 0, 0))]
        + [full(a) for a in consts_in] + [pl.BlockSpec(memory_space=pl.ANY)],
        out_specs=pl.BlockSpec((n_tok, NSA_HEADS * NSA_HD), lambda s, c, pt: (blk0 + s, 0)),
        scratch_shapes=[pltpu.VMEM((nb, LANES), F32), pltpu.VMEM((hq, LANES), BF16), pltpu.VMEM((nb, hq), F32),
                        pltpu.VMEM((64, hq), F32), pltpu.VMEM((64, hq), F32), pltpu.VMEM((1, hq), F32),
                        pltpu.VMEM((1, hq), F32), pltpu.VMEM((LANES, hq), F32)],
    )
    return pl.pallas_call(
        functools.partial(_nsa_decode_kernel, pps=pps, past_len=past_len),
        grid_spec=grid_spec,
        out_shape=jax.ShapeDtypeStruct(prev.shape, BF16),
        input_output_aliases={len(ins) - 1: 0},
        compiler_params=_params(("parallel", "arbitrary")),
    )(*ins)


def _rope_tables(pos, half):
    inv = ROPE_THETA ** (-jnp.arange(half, dtype=F32) / half)
    ang = pos.astype(F32)[:, None] * inv
    return jnp.cos(ang), jnp.sin(ang)


def _tables64(pos):
    c, s = _rope_tables(pos, NSA_HD // 2)
    return jnp.concatenate([c, c, c, c], axis=1), jnp.concatenate([-s, s, -s, s], axis=1)


def _tables_mla(pos):
    c, s = _rope_tables(pos, MLA_ROPE // 2)
    n = pos.shape[0]
    cm = jnp.concatenate([jnp.ones((n, 64), F32), c, c, jnp.ones((n, 32), F32)], axis=1)
    sm = jnp.concatenate([jnp.zeros((n, 64), F32), -s, s, jnp.zeros((n, 32), F32)], axis=1)
    return cm, sm


def _np_consts(n_tok, nb_prompt, seq, tk, nb_past, past_len, dec_rows):
    hq = 8 * n_tok
    ones64 = np.zeros((512, hq), np.float32)
    seln = np.zeros((1024, 512), np.float32)
    selr = np.zeros((1024, MLA_ROPE), np.float32)
    selb = np.zeros((1024, MLA_ROPE), np.float32)
    sel512 = np.zeros((512, LANES), np.float32)
    rep = np.zeros((64, 512), np.float32)
    hsum = np.zeros((hq, hq), np.float32)
    pick = np.zeros((3, LANES, hq), np.float32)
    for h in range(8):
        ones64[64 * h:64 * (h + 1), n_tok * h:n_tok * (h + 1)] = 1.0
        for dd in range(64):
            seln[128 * h + dd, 64 * h + dd] = 1.0
            sel512[64 * h + dd, dd] = 1.0
            rep[dd, 64 * h + dd] = 1.0
        for e in range(MLA_ROPE):
            selr[128 * h + 64 + e, e] = 1.0
            if e < MLA_ROPE // 2:
                selb[128 * h + 64 + e + MLA_ROPE // 2, e] = 1.0
            else:
                selb[128 * h + 64 + e - MLA_ROPE // 2, e] = -1.0
        for b in range(3):
            pick[b, 3 * h + b, n_tok * h:n_tok * (h + 1)] = 1.0
        for t in range(n_tok):
            hsum[n_tok * h + t, t::n_tok] = 1.0
    key_blk = np.arange(seq) // CMP_BLOCK
    e_prompt = (np.arange(nb_prompt)[None, :, None] == key_blk.reshape(seq // tk, 1, tk)).astype(np.float32)
    key_blk_d = np.arange(past_len) // CMP_BLOCK
    e_dec = (key_blk_d.reshape(past_len // dec_rows, dec_rows, 1) == np.arange(nb_past)[None, None, :]).astype(np.float32)
    raw = dict(ones64=ones64, seln=seln, selr=selr, selb=selb, sel512=sel512, rep=rep, hsum=hsum, pick=pick,
               e_prompt=e_prompt, e_dec=e_dec)
    return {k: jnp.asarray(v, BF16) for k, v in raw.items()}


def _prep_weights(w_in, w_q_b, w_kv_b, w_out, w_mem_q, w_mem_k, w_mem_v, w_mem_o, w_up, w_down, w_cmp_pool):
    depth, d, _ = w_in.shape
    z = lambda n: jnp.zeros((depth, d, n), w_in.dtype)
    kr = w_in[..., 384:416]
    win = jnp.concatenate([w_in[..., 0:384], kr, z(32), kr, z(32), w_in[..., 416:1336], z(IN_PAD - SEG_GL - 24)], axis=-1)
    wqb = jnp.pad(w_q_b.reshape(depth, -1, MLA_HEADS, MLA_QK), ((0, 0), (0, 0), (0, 0), (0, LANES - MLA_QK)))
    wqb = wqb.reshape(depth, -1, MLA_HEADS * LANES)
    kvb = w_kv_b.reshape(depth, -1, MLA_HEADS, MLA_NOPE + MLA_V)
    wk, wv = kvb[..., :MLA_NOPE], kvb[..., MLA_NOPE:]
    r = wk.shape[1]
    wk_pad = jnp.pad(wk, ((0, 0), (0, 0), (0, 0), (0, LANES - MLA_NOPE))).reshape(depth, r, -1)
    wv_even = jnp.pad(wv, ((0, 0), (0, 0), (0, 0), (0, LANES - MLA_V)))
    wv_odd = jnp.pad(wv, ((0, 0), (0, 0), (0, 0), (LANES - MLA_V, 0)))
    odd = (jnp.arange(MLA_HEADS) % 2 == 1)[None, None, :, None]
    wv_pair = jnp.where(odd, wv_odd, wv_even).reshape(depth, r, -1)
    wk_c = wk.reshape(depth, r, -1)
    wp = jnp.concatenate([jnp.repeat(w_cmp_pool[:, 0, :, None], NSA_HD, axis=-1),
                          jnp.repeat(w_cmp_pool[:, 1, :, None], NSA_HD, axis=-1)], axis=-1)
    b16 = lambda a: a.astype(BF16)
    return dict(w_in=b16(win), w_q_b=b16(wqb), wk_pad=b16(wk_pad), wv_pair=b16(wv_pair), wk_c=b16(wk_c),
                wk_ct=b16(jnp.swapaxes(wk_c, 1, 2)), wv_c=b16(wv.reshape(depth, r, -1)),
                w_out=b16(w_out), w_mem_q=b16(w_mem_q), w_mem_kv=b16(jnp.concatenate([w_mem_k, w_mem_v], axis=-1)),
                w_mem_o=b16(w_mem_o), w_up=b16(w_up), w_down=b16(w_down), w_pool=wp)


def _pad_lanes(g, n):
    return jnp.pad(g, ((0, 0), (0, n - g.shape[1])))


def kernel(x_prompt, x_sample, cache_mla, cache_nsa_cmp, cache_nsa_sel, state_nsa_win, cache_mem, page_table,
           mem_prompt, g_mix, w_in, b_gate, g_q_a, w_q_b, g_kv_a, w_kv_b, g_mla_q, g_mla_k, g_nsa_q, g_cmp_k,
           g_sel_k, g_win_k, w_cmp_pool, w_out, g_mem, g_mem_ctx, w_mem_q, w_mem_k, w_mem_v, g_mem_q, g_mem_k,
           w_mem_o, g_mlp, w_up, w_down):
    bsz, seq, d = x_prompt.shape
    n_seq, n_tok, _ = x_sample.shape
    depth = w_in.shape[0]
    n_pages, page_rows = page_table.shape[1], cache_mla.shape[2]
    past_len = n_pages * page_rows
    w_buf = state_nsa_win.shape[2]
    mem_len = mem_prompt.shape[1]
    n_win = min(WINDOW, seq)
    assert seq % 128 == 0 and seq >= WINDOW + 128 and past_len % CMP_BLOCK == 0 and n_tok == 8
    tp, ts_ = bsz * seq, n_seq * n_tok
    t_all = tp + ts_
    tm = _pick(t_all, (512, 256, 128))
    tm_mlp = _pick(t_all, (1024, 512, 256, 128))
    tq_mla = _pick(seq, (512, 256, 128))
    tq_nsa, tk_nsa = 128, _pick(seq, (512, 256, 128))
    pps = _pick(n_pages, (16, 8, 4, 2, 1))
    mem_group = _pick(n_seq, (8, 4, 2, 1))
    assert tp % (mem_group * n_tok) == 0
    nb_p, nb_s = seq // CMP_BLOCK, past_len // CMP_BLOCK

    pos_p = jnp.tile(jnp.arange(seq), bsz)
    pos_s = jnp.tile(past_len + jnp.arange(n_tok), n_seq)
    pos_all = jnp.concatenate([pos_p, pos_s])
    c64, s64 = _tables64(pos_all)
    cm, sm = _tables_mla(pos_all)
    tabs = dict(c64=c64, s64=s64, cm=cm, sm=sm)
    ce_p, se_p = _tables64(jnp.arange(nb_p) * CMP_BLOCK + (CMP_BLOCK - 1))
    ce_s, se_s = _tables64(jnp.arange(nb_s) * CMP_BLOCK + (CMP_BLOCK - 1))
    c16, s16 = _rope_tables(jnp.arange(past_len + n_tok), MLA_ROPE // 2)
    c32, s32 = jnp.concatenate([c16, c16], axis=1), jnp.concatenate([s16, s16], axis=1)
    dtabs = dict(tc=c32[:past_len], ts=s32[:past_len], tcn=c32[past_len:], tsn=s32[past_len:])
    consts = _np_consts(n_tok, nb_p, seq, tk_nsa, nb_s, past_len, pps * page_rows)

    pw = _prep_weights(w_in, w_q_b, w_kv_b, w_out, w_mem_q, w_mem_k, w_mem_v, w_mem_o, w_up, w_down, w_cmp_pool)
    gk_pad = _pad_lanes(g_mla_k, LANES)
    gq_pad = _pad_lanes(g_mla_q, LANES)
    two = lambda g: jnp.concatenate([g, g], axis=1)
    bg_pad = _pad_lanes(b_gate, LANES)
    gn_tiled = jnp.tile(g_mla_k[:, :MLA_NOPE], (1, MLA_HEADS))

    n_pool = cache_mla.shape[1]
    cache_mla4 = jnp.swapaxes(cache_mla, 2, 3)
    cache_cmp4 = jnp.transpose(cache_nsa_cmp, (0, 1, 3, 4, 2)).reshape(depth, n_pool, LANES, page_rows)
    cache_sel4 = jnp.transpose(cache_nsa_sel, (0, 1, 3, 4, 2)).reshape(depth, n_pool, LANES, page_rows)
    win_state4 = jnp.transpose(state_nsa_win, (0, 1, 3, 4, 2)).reshape(depth, n_seq, LANES, w_buf)
    cache_mem4 = jnp.transpose(cache_mem, (0, 1, 3, 4, 5, 2)).reshape(depth, n_seq, 2 * MEM_HEADS * MEM_HD, mem_len)
    mem_flat = mem_prompt.reshape(bsz * mem_len, d)

    x = jnp.concatenate([x_prompt.reshape(tp, d), x_sample.reshape(ts_, d)], axis=0)
    outs = {k: [] for k in ("p_mla", "p_cmp", "p_sel", "p_win", "p_mem", "s_mla", "s_cmp", "s_sel", "s_win")}
    row = lambda a, l: a[l][None, :]
    for l in range(depth):
        w = dict(g_mix=row(g_mix, l), w_in=pw["w_in"][l], g_q_a=row(g_q_a, l), w_q_b=pw["w_q_b"][l],
                 g_kv_a=row(g_kv_a, l), wk_pad=pw["wk_pad"][l], wv_pair=pw["wv_pair"][l], g_mla_q=row(gq_pad, l),
                 g_mla_k=row(gk_pad, l), g_nsa_q=row(two(g_nsa_q), l), g_sel_k=row(two(g_sel_k), l),
                 g_win_k=row(two(g_win_k), l), b_gate=row(bg_pad, l), wk_c=pw["wk_c"][l], wk_ct=pw["wk_ct"][l],
                 wv_c=pw["wv_c"][l], g_kn=row(gn_tiled, l), g_kr=row(g_mla_k[:, MLA_NOPE:], l))
        qm, lat, km, vm, qn, cmp_r, sel_r, win_r, selkv, winkv, gates = _proj_call(x, w, tabs, tm)

        g_cmp2 = row(two(g_cmp_k), l)
        o_m = _mla_attn_call(qm, km, vm, bsz, seq, tq_mla)
        o_m = _mla_decode_call(page_table, qm, lat, o_m, tp, n_tok, cache_mla4, l, dtabs, w, consts, pps)
        cmpx_p = _cmp_pool_prompt_call(cmp_r, pw["w_pool"][l], g_cmp2, ce_p, se_p, bsz, seq)
        o_n = _nsa_prompt_call(qn, gates, cmpx_p, selkv, winkv, consts["e_prompt"], bsz, seq, tq_nsa, tk_nsa)
        o_n = _nsa_decode_call(page_table, qn, gates, win_r, sel_r, o_n, tp, n_tok, win_state4, cache_cmp4,
                               cache_sel4, l, pw["w_pool"][l], g_cmp2, ce_s, se_s, consts, pps, past_len)
        nm = MLA_HEADS * MLA_V
        x = _mm_res_call(x, [(o_m, pw["w_out"][l][:nm]), (o_n, pw["w_out"][l][nm:])], tm)

        mkv = _memkv_call(mem_flat, row(g_mem_ctx, l), pw["w_mem_kv"][l], row(two(g_mem_k), l))
        q_mem = _norm_mm_call(x, row(g_mem, l), pw["w_mem_q"][l], tm)
        gq2 = row(two(g_mem_q), l)
        o_mem = _mem_attn_prompt_call(q_mem, mkv.reshape(bsz, mem_len, -1), gq2, bsz, seq,
                                      _pick(seq, (512, 256, 128)))
        o_mem = _mem_attn_sample_call(q_mem, cache_mem4, l, gq2, o_mem, tp, n_seq, n_tok, mem_group)
        x = _mm_res_call(x, [(o_mem, pw["w_mem_o"][l])], tm)

        x = _mlp_call(x, row(g_mlp, l), pw["w_up"][l], pw["w_down"][l], tm_mlp)

        outs["p_mla"].append(lat[:tp].reshape(bsz, seq, -1))
        outs["p_cmp"].append(cmp_r[:tp].reshape(bsz, seq, 2, NSA_HD))
        outs["p_sel"].append(sel_r[:tp].reshape(bsz, seq, 2, NSA_HD))
        outs["p_win"].append(win_r[:tp].reshape(bsz, seq, 2, NSA_HD)[:, seq - n_win:])
        outs["p_mem"].append(mkv.reshape(bsz, mem_len, 2, MEM_HEADS, MEM_HD))
        outs["s_mla"].append(lat[tp:].reshape(n_seq, n_tok, -1))
        outs["s_cmp"].append(cmp_r[tp:].reshape(n_seq, n_tok, 2, NSA_HD))
        outs["s_sel"].append(sel_r[tp:].reshape(n_seq, n_tok, 2, NSA_HD))
        win_all = jnp.concatenate([state_nsa_win[l], win_r[tp:].reshape(n_seq, n_tok, 2, NSA_HD)], axis=1)
        outs["s_win"].append(win_all[:, win_all.shape[1] - w_buf:])

    st = lambda k: jnp.stack(outs[k])
    return (x[:tp].reshape(bsz, seq, d), x[tp:].reshape(n_seq, n_tok, d), st("p_mla"), st("p_cmp"), st("p_sel"),
            st("p_win"), st("p_mem"), st("s_mla"), st("s_cmp"), st("s_sel"), st("s_win"))
```
